```python
import math
import jax
import jax.numpy as jnp
from jax import lax
import numpy as np

D_MODEL = 1024
BATCH = 8
SEQ = 8192
DEPTH = 2

ROPE_THETA = 10000.0
NORM_EPS = 1e-6
Q_BLOCK = 128
NEG_INF = -1e30
N_BRANCHES = 3

DIFF_HEADS = 4
DIFF_HEAD_DIM = 64
DIFF_V_DIM = 2 * DIFF_HEAD_DIM
DIFF_WIDTH = DIFF_HEADS * DIFF_V_DIM

DIL_PATTERNS = ((128, 1), (512, 4), (2048, 16))
DIL_HEADS = 4
DIL_HEAD_DIM = 64
DIL_WIDTH = DIL_HEADS * DIL_HEAD_DIM

MLA_HEADS = 8
MLA_NOPE_DIM = 64
MLA_ROPE_DIM = 32
MLA_V_DIM = 64
MLA_Q_LORA = 768
MLA_KV_LORA = 256
MLA_WIDTH = MLA_HEADS * MLA_V_DIM

MLP_HIDDEN = 4 * D_MODEL

IN_COLS = (2 * DIFF_HEADS * 2 * DIFF_HEAD_DIM + DIFF_WIDTH
           + len(DIL_PATTERNS) * 3 * DIL_HEADS * DIL_HEAD_DIM
           + MLA_Q_LORA + MLA_KV_LORA + MLA_ROPE_DIM
           + N_BRANCHES * D_MODEL)

kernel_name = "hybrid_gated_diff_dilated_mla_encoder"


def _in_layout():
    segs = [("diff_q", DIFF_HEADS * 2 * DIFF_HEAD_DIM),
            ("diff_k", DIFF_HEADS * 2 * DIFF_HEAD_DIM),
            ("diff_v", DIFF_WIDTH)]
    for g in range(len(DIL_PATTERNS)):
        for t in ("q", "k", "v"):
            segs.append(("dil%d_%s" % (g, t), DIL_HEADS * DIL_HEAD_DIM))
    segs += [("mla_cq", MLA_Q_LORA), ("mla_ckv", MLA_KV_LORA), ("mla_kr", MLA_ROPE_DIM)]
    segs += [("gate%d" % i, D_MODEL) for i in range(N_BRANCHES)]
    layout, off = {}, 0
    for name, w in segs:
        layout[name] = (off, off + w)
        off += w
    return layout


def _rmsnorm(x, g):
    x32 = x.astype(jnp.float32)
    y = x32 * lax.rsqrt(jnp.mean(x32 * x32, axis=-1, keepdims=True) + NORM_EPS)
    return (y * g.astype(jnp.float32)).astype(x.dtype)


def _rope(x, pos):
    half = x.shape[-1] // 2
    inv = ROPE_THETA ** (-jnp.arange(half, dtype=jnp.float32) / half)
    ang = pos[:, None] * inv[None, :]
    cos = jnp.cos(ang)[None, :, None, :].astype(x.dtype)
    sin = jnp.sin(ang)[None, :, None, :].astype(x.dtype)
    x1, x2 = x[..., :half], x[..., half:]
    return jnp.concatenate([x1 * cos - x2 * sin, x2 * cos + x1 * sin], axis=-1)


def _to_qblocks(t):
    b, s = t.shape[:2]
    return t.reshape((b, s // Q_BLOCK, Q_BLOCK) + t.shape[2:]).swapaxes(0, 1)


def _from_qblocks(t):
    nb, b, qb = t.shape[:3]
    return t.swapaxes(0, 1).reshape((b, nb * qb) + t.shape[3:])


def _dense_attention(q, k, v, scale):
    def one(qb):
        s = jnp.einsum('bqhd,bkhd->bhqk', qb, k).astype(jnp.float32) * scale
        p = jax.nn.softmax(s, axis=-1).astype(v.dtype)
        return jnp.einsum('bhqk,bkhd->bqhd', p, v)
    return _from_qblocks(lax.map(one, _to_qblocks(q)))


def _differential_attention(q1, q2, k1, k2, v, lam):
    scale = DIFF_HEAD_DIM ** -0.5
    def one(qs):
        qb1, qb2 = qs
        p1 = jax.nn.softmax(jnp.einsum('bqhd,bkhd->bhqk', qb1, k1).astype(jnp.float32) * scale, axis=-1)
        p2 = jax.nn.softmax(jnp.einsum('bqhd,bkhd->bhqk', qb2, k2).astype(jnp.float32) * scale, axis=-1)
        a = (p1 - lam * p2).astype(v.dtype)
        return jnp.einsum('bhqk,bkhd->bqhd', a, v)
    return _from_qblocks(lax.map(one, (_to_qblocks(q1), _to_qblocks(q2))))


def _dilated_group(q, k, v, window, dilation):
    b, s_len, h, dh = q.shape
    span = window // (2 * dilation)
    blk = span
    n = -(-s_len // (dilation * blk)) * blk
    lp = n * dilation
    nb = n // blk
    pad = lp - s_len

    def blocks(t):
        t = jnp.pad(t, ((0, 0), (0, pad), (0, 0), (0, 0)))
        return t.reshape(b, nb, blk, dilation, h, dh)

    def windows(t):
        tp = jnp.pad(t, [(0, 0), (1, 1)] + [(0, 0)] * (t.ndim - 2))
        return jnp.concatenate([tp[:, :-2], tp[:, 1:-1], tp[:, 2:]], axis=2)

    qb = blocks(q)
    kw = windows(blocks(k))
    vw = windows(blocks(v))
    valid = (jnp.arange(lp) < s_len).reshape(1, nb, blk, dilation)
    valid_w = windows(valid)

    sc = jnp.einsum('bnqchd,bnkchd->bnchqk', qb, kw).astype(jnp.float32) * (dh ** -0.5)
    rel = jnp.arange(3 * blk)[None, :] - blk - jnp.arange(blk)[:, None]
    band = jnp.abs(rel) <= span
    mask = band[None, None, None, None] & valid_w.transpose(0, 1, 3, 2)[:, :, :, None, None, :]
    sc = jnp.where(mask, sc, NEG_INF)
    m = jnp.max(sc, axis=-1, keepdims=True)
    p = jnp.exp(sc - m)
    l = jnp.sum(p, axis=-1, keepdims=True)
    o = jnp.einsum('bnchqk,bnkchd->bnqchd', (p / l).astype(v.dtype), vw)
    lse = (m + jnp.log(l))[..., 0]
    o = o.reshape(b, lp, h, dh)[:, :s_len]
    lse = lse.transpose(0, 1, 4, 2, 3).reshape(b, lp, h)[:, :s_len]
    return o, lse


def _mixer(h, layer_idx, w_in, b_gate, diff_lambda, g_diff, g_cq, g_ckv, w_uq, w_ukv,
           w_o_diff, w_o_dil, w_o_mla, w_out):
    b, s_len, _ = h.shape
    pos = jnp.arange(s_len, dtype=jnp.float32)
    layout = _in_layout()

    def proj(name):
        a, e = layout[name]
        return h @ w_in[:, a:e]

    q = _rope(proj("diff_q").reshape(b, s_len, 2 * DIFF_HEADS, DIFF_HEAD_DIM), pos)
    k = _rope(proj("diff_k").reshape(b, s_len, 2 * DIFF_HEADS, DIFF_HEAD_DIM), pos)
    q = q.reshape(b, s_len, DIFF_HEADS, 2, DIFF_HEAD_DIM)
    k = k.reshape(b, s_len, DIFF_HEADS, 2, DIFF_HEAD_DIM)
    v = proj("diff_v").reshape(b, s_len, DIFF_HEADS, DIFF_V_DIM)
    lam_init = 0.8 - 0.6 * math.exp(-0.3 * layer_idx)
    lp32 = diff_lambda.astype(jnp.float32)
    lam = jnp.exp(jnp.sum(lp32[0] * lp32[1])) - jnp.exp(jnp.sum(lp32[2] * lp32[3])) + lam_init
    o_a = _differential_attention(q[..., 0, :], q[..., 1, :], k[..., 0, :], k[..., 1, :], v, lam)
    o_a = _rmsnorm(o_a, g_diff) * (1.0 - lam_init)
    y_a = o_a.reshape(b, s_len, DIFF_WIDTH) @ w_o_diff

    outs, lses = [], []
    for g, (window, dilation) in enumerate(DIL_PATTERNS):
        qg = _rope(proj("dil%d_q" % g).reshape(b, s_len, DIL_HEADS, DIL_HEAD_DIM), pos)
        kg = _rope(proj("dil%d_k" % g).reshape(b, s_len, DIL_HEADS, DIL_HEAD_DIM), pos)
        vg = proj("dil%d_v" % g).reshape(b, s_len, DIL_HEADS, DIL_HEAD_DIM)
        o_g, lse_g = _dilated_group(qg, kg, vg, window, dilation)
        outs.append(o_g)
        lses.append(lse_g)
    alpha = jax.nn.softmax(jnp.stack(lses, axis=0), axis=0)
    o_b = jnp.sum(alpha[..., None].astype(outs[0].dtype) * jnp.stack(outs, axis=0), axis=0)
    y_b = o_b.reshape(b, s_len, DIL_WIDTH) @ w_o_dil

    c_q = _rmsnorm(proj("mla_cq"), g_cq)
    c_kv = _rmsnorm(proj("mla_ckv"), g_ckv)
    k_rope = _rope(proj("mla_kr").reshape(b, s_len, 1, MLA_ROPE_DIM), pos)
    qh = (c_q @ w_uq).reshape(b, s_len, MLA_HEADS, MLA_NOPE_DIM + MLA_ROPE_DIM)
    q_c = jnp.concatenate([qh[..., :MLA_NOPE_DIM], _rope(qh[..., MLA_NOPE_DIM:], pos)], axis=-1)
    kv = (c_kv @ w_ukv).reshape(b, s_len, MLA_HEADS, MLA_NOPE_DIM + MLA_V_DIM)
    k_c = jnp.concatenate([kv[..., :MLA_NOPE_DIM],
                           jnp.broadcast_to(k_rope, (b, s_len, MLA_HEADS, MLA_ROPE_DIM))], axis=-1)
    v_c = kv[..., MLA_NOPE_DIM:]
    o_c = _dense_attention(q_c, k_c, v_c, (MLA_NOPE_DIM + MLA_ROPE_DIM) ** -0.5)
    y_c = o_c.reshape(b, s_len, MLA_WIDTH) @ w_o_mla

    merged = (jax.nn.sigmoid(proj("gate0") + b_gate[0]) * y_a
              + jax.nn.sigmoid(proj("gate1") + b_gate[1]) * y_b
              + jax.nn.sigmoid(proj("gate2") + b_gate[2]) * y_c)
    return merged @ w_out


def setup_inputs(seed: int = 0) -> dict:
    key = jax.random.key(seed)
    ks = jax.random.split(key, 20)

    def nrm(k, shape, fan_in):
        return jax.random.normal(k, shape, jnp.float32) * (fan_in ** -0.5)

    def gain(k, shape):
        return 1.0 + 0.05 * jax.random.normal(k, shape, jnp.float32)

    return {
        "x": jax.random.normal(ks[0], (BATCH, SEQ, D_MODEL), jnp.float32),
        "w_in": nrm(ks[1], (DEPTH, D_MODEL, IN_COLS), D_MODEL),
        "b_gate": 0.01 * jax.random.normal(ks[2], (DEPTH, N_BRANCHES, D_MODEL), jnp.float32),
        "g_mix": gain(ks[3], (DEPTH, D_MODEL)),
        "diff_lambda": 0.1 * jax.random.normal(ks[4], (DEPTH, 4, DIFF_HEAD_DIM), jnp.float32),
        "g_diff": gain(ks[5], (DEPTH, DIFF_V_DIM)),
        "g_cq": gain(ks[6], (DEPTH, MLA_Q_LORA)),
        "g_ckv": gain(ks[7], (DEPTH, MLA_KV_LORA)),
        "w_uq": nrm(ks[8], (DEPTH, MLA_Q_LORA, MLA_HEADS * (MLA_NOPE_DIM + MLA_ROPE_DIM)), MLA_Q_LORA),
        "w_ukv": nrm(ks[9], (DEPTH, MLA_KV_LORA, MLA_HEADS * (MLA_NOPE_DIM + MLA_V_DIM)), MLA_KV_LORA),
        "w_o_diff": nrm(ks[10], (DEPTH, DIFF_WIDTH, D_MODEL), DIFF_WIDTH),
        "w_o_dil": nrm(ks[11], (DEPTH, DIL_WIDTH, D_MODEL), DIL_WIDTH),
        "w_o_mla": nrm(ks[12], (DEPTH, MLA_WIDTH, D_MODEL), MLA_WIDTH),
        "w_out": nrm(ks[13], (DEPTH, D_MODEL, D_MODEL), D_MODEL),
        "g_mlp": gain(ks[14], (DEPTH, D_MODEL)),
        "w_up": nrm(ks[15], (DEPTH, D_MODEL, MLP_HIDDEN), D_MODEL),
        "w_down": nrm(ks[16], (DEPTH, MLP_HIDDEN, D_MODEL), MLP_HIDDEN),
        "g_final": gain(ks[17], (D_MODEL,)),
    }


def reference(x, w_in, b_gate, g_mix, diff_lambda, g_diff, g_cq, g_ckv, w_uq, w_ukv,
              w_o_diff, w_o_dil, w_o_mla, w_out, g_mlp, w_up, w_down, g_final):
    for l in range(DEPTH):
        h = _rmsnorm(x, g_mix[l])
        x = x + _mixer(h, l, w_in[l], b_gate[l], diff_lambda[l], g_diff[l], g_cq[l], g_ckv[l],
                       w_uq[l], w_ukv[l], w_o_diff[l], w_o_dil[l], w_o_mla[l], w_out[l])
        h = _rmsnorm(x, g_mlp[l])
        x = x + jnp.square(jax.nn.relu(h @ w_up[l])) @ w_down[l]
    return _rmsnorm(x, g_final)
```

```python
import functools
import math

import jax
import jax.numpy as jnp
from jax import lax
from jax.experimental import pallas as pl
from jax.experimental.pallas import tpu as pltpu

F32 = jnp.float32
BF16 = jnp.bfloat16

D_MODEL = 1024
ROPE_THETA = 10000.0
NORM_EPS = 1e-6
NEG_BIG = -1e30
LOG2E = math.log2(math.e)
LN2 = math.log(2.0)

DIFF_HEADS = 4
DIFF_HEAD_DIM = 64
DIFF_V_DIM = 128
DIFF_WIDTH = 512

DIL_PATTERNS = ((128, 1), (512, 4), (2048, 16))
DIL_SPAN = 64
DIL_HEADS = 4
DIL_HEAD_DIM = 64
DIL_WIDTH = 256

MLA_HEADS = 8
MLA_NOPE_DIM = 64
MLA_ROPE_DIM = 32
MLA_V_DIM = 64
MLA_Q_LORA = 768
MLA_KV_LORA = 256
MLA_WIDTH = 512
MLA_PAD_DIM = 128

MLP_HIDDEN = 4096
LANES = 128

QKV_GROUP = 1280
VMEM_LIMIT = 56 * 1024 * 1024


def _params(semantics):
    return pltpu.CompilerParams(dimension_semantics=semantics, vmem_limit_bytes=VMEM_LIMIT)


def _rmsnorm(x, g):
    ms = jnp.mean(x * x, axis=-1, keepdims=True)
    return x * lax.rsqrt(ms + NORM_EPS) * g


def _rope_chunk(a, c, sa, sb, half):
    return a * c + pltpu.roll(a, LANES - half, 1) * sa + pltpu.roll(a, half, 1) * sb


def _qkv_kernel(x_ref, g_ref, w_ref, c_ref, sa_ref, sb_ref, o_ref, h_ref, *, qscale):
    j = pl.program_id(1)

    @pl.when(j == 0)
    def _():
        h_ref[...] = _rmsnorm(x_ref[...], g_ref[...]).astype(BF16)

    acc = jnp.dot(h_ref[...], w_ref[...], preferred_element_type=F32)

    @pl.when(j == 2)
    def _():
        o_ref[...] = acc.astype(BF16)

    @pl.when(j < 2)
    def _():
        scale = jnp.where(j == 0, qscale, 1.0).astype(F32)
        c = c_ref[...] * scale
        sa = sa_ref[...] * scale
        sb = sb_ref[...] * scale
        for ch in range(QKV_GROUP // LANES):
            sl = slice(ch * LANES, (ch + 1) * LANES)
            o_ref[:, sl] = _rope_chunk(acc[:, sl], c, sa, sb, DIFF_HEAD_DIM // 2).astype(BF16)


def _qkv_proj(x2, g, w, tabs, seq, tm):
    t = x2.shape[0]
    nrow = seq // tm
    tab_spec = pl.BlockSpec((tm, LANES), lambda i, j: (i % nrow, 0))
    return pl.pallas_call(
        functools.partial(_qkv_kernel, qscale=DIFF_HEAD_DIM ** -0.5 * LOG2E),
        out_shape=jax.ShapeDtypeStruct((t, 3 * QKV_GROUP), BF16),
        grid=(t // tm, 3),
        in_specs=[pl.BlockSpec((tm, D_MODEL), lambda i, j: (i, 0)),
                  pl.BlockSpec((1, D_MODEL), lambda i, j: (0, 0)),
                  pl.BlockSpec((D_MODEL, QKV_GROUP), lambda i, j: (0, j)),
                  tab_spec, tab_spec, tab_spec],
        out_specs=pl.BlockSpec((tm, QKV_GROUP), lambda i, j: (i, j)),
        scratch_shapes=[pltpu.VMEM((tm, D_MODEL), BF16)],
        compiler_params=_params(("parallel", "arbitrary")),
        name="qkv_proj",
    )(x2, g, w, *tabs)


def _gate_kernel(x_ref, g_ref, w_ref, b_ref, o_ref, h_ref):
    @pl.when(pl.program_id(1) == 0)
    def _():
        h_ref[...] = _rmsnorm(x_ref[...], g_ref[...]).astype(BF16)

    z = jnp.dot(h_ref[...], w_ref[...], preferred_element_type=F32) + b_ref[0]
    o_ref[...] = (1.0 / (1.0 + jnp.exp(-z))).astype(BF16)


def _gate_proj(x2, g, w, b, tm):
    t = x2.shape[0]
    return pl.pallas_call(
        _gate_kernel,
        out_shape=jax.ShapeDtypeStruct((t, 3 * D_MODEL), BF16),
        grid=(t // tm, 3),
        in_specs=[pl.BlockSpec((tm, D_MODEL), lambda i, j: (i, 0)),
                  pl.BlockSpec((1, D_MODEL), lambda i, j: (0, 0)),
                  pl.BlockSpec((D_MODEL, D_MODEL), lambda i, j: (0, j)),
                  pl.BlockSpec((1, 1, D_MODEL), lambda i, j: (j, 0, 0))],
        out_specs=pl.BlockSpec((tm, D_MODEL), lambda i, j: (i, j)),
        scratch_shapes=[pltpu.VMEM((tm, D_MODEL), BF16)],
        compiler_params=_params(("parallel", "arbitrary")),
        name="gate_proj",
    )(x2, g, w, b)


def _mla_pro_kernel(x_ref, g_ref, wcq_ref, wckv_ref, wkr_ref, gcq_ref, gckv_ref, wuq_ref, wuk_ref, wuv_ref,
                    c_ref, sa_ref, sb_ref, q_ref, k_ref, v_ref, *, qscale):
    h = _rmsnorm(x_ref[...], g_ref[...]).astype(BF16)
    cq = jnp.dot(h, wcq_ref[...], preferred_element_type=F32)
    cq = _rmsnorm(cq, gcq_ref[...]).astype(BF16)
    ckv = jnp.dot(h, wckv_ref[...], preferred_element_type=F32)
    ckv = _rmsnorm(ckv, gckv_ref[...]).astype(BF16)
    c, sa, sb = c_ref[...], sa_ref[...], sb_ref[...]
    half = MLA_ROPE_DIM // 2
    kr = _rope_chunk(jnp.dot(h, wkr_ref[...], preferred_element_type=F32), c, sa, sb, half)
    qh = jnp.dot(cq, wuq_ref[...], preferred_element_type=F32)
    kn = jnp.dot(ckv, wuk_ref[...], preferred_element_type=F32)
    cq_s, sa_s, sb_s = c * qscale, sa * qscale, sb * qscale
    for hd in range(MLA_HEADS):
        sl = slice(hd * MLA_PAD_DIM, (hd + 1) * MLA_PAD_DIM)
        q_ref[:, sl] = _rope_chunk(qh[:, sl], cq_s, sa_s, sb_s, half).astype(BF16)
        k_ref[:, sl] = (kn[:, sl] + kr).astype(BF16)
    v_ref[...] = jnp.dot(ckv, wuv_ref[...], preferred_element_type=F32).astype(BF16)


def _mla_prologue(x2, g, wcq, wckv, wkr, gcq, gckv, wuq, wuk, wuv, tabs, seq, tm):
    t = x2.shape[0]
    nrow = seq // tm
    full = lambda a: pl.BlockSpec(a.shape, lambda i: (0,) * a.ndim)
    tab_spec = pl.BlockSpec((tm, LANES), lambda i: (i % nrow, 0))
    hp = MLA_HEADS * MLA_PAD_DIM
    return pl.pallas_call(
        functools.partial(_mla_pro_kernel, qscale=(MLA_NOPE_DIM + MLA_ROPE_DIM) ** -0.5 * LOG2E),
        out_shape=(jax.ShapeDtypeStruct((t, hp), BF16), jax.ShapeDtypeStruct((t, hp), BF16),
                   jax.ShapeDtypeStruct((t, MLA_WIDTH), BF16)),
        grid=(t // tm,),
        in_specs=[pl.BlockSpec((tm, D_MODEL), lambda i: (i, 0)), full(g), full(wcq), full(wckv), full(wkr),
                  full(gcq), full(gckv), full(wuq), full(wuk), full(wuv), tab_spec, tab_spec, tab_spec],
        out_specs=(pl.BlockSpec((tm, hp), lambda i: (i, 0)), pl.BlockSpec((tm, hp), lambda i: (i, 0)),
                   pl.BlockSpec((tm, MLA_WIDTH), lambda i: (i, 0))),
        compiler_params=_params(("parallel",)),
        name="mla_prologue",
    )(x2, g, wcq, wckv, wkr, gcq, gckv, wuq, wuk, wuv, *tabs)


def _online_step(q, ks, vs, m_ref, l_ref, a_ref):
    s = lax.dot_general(q, ks, (((1,), (1,)), ((), ())), preferred_element_type=F32)
    tk = s.shape[1]
    m_prev = m_ref[...]
    m_new = jnp.maximum(m_prev, jnp.max(s, axis=1, keepdims=True))
    alpha = jnp.exp2(m_prev - m_new)
    p = jnp.exp2(s - pltpu.repeat(m_new, tk // LANES, axis=1))
    psum = p[:, :LANES]
    for c in range(1, tk // LANES):
        psum = psum + p[:, c * LANES:(c + 1) * LANES]
    m_ref[...] = m_new
    l_ref[...] = alpha * l_ref[...] + psum
    a_ref[...] = alpha * a_ref[...] + jnp.dot(p.astype(BF16), vs, preferred_element_type=F32)


def _init_state(refs):
    for m_ref, l_ref, a_ref in refs:
        m_ref[...] = jnp.full(m_ref.shape, NEG_BIG, F32)
        l_ref[...] = jnp.zeros(l_ref.shape, F32)
        a_ref[...] = jnp.zeros(a_ref.shape, F32)


def _normalised(l_ref, a_ref):
    return a_ref[...] / jnp.sum(l_ref[...], axis=1, keepdims=True)


def _diff_attn_kernel(lam_ref, gd_ref, q_ref, k_ref, v_ref, o_ref, m1, l1, a1, m2, l2, a2, *, tk, lam_init):
    q = q_ref[0]
    lane = lax.broadcasted_iota(jnp.int32, q.shape, 1)
    q1 = jnp.where(lane < DIFF_HEAD_DIM, q, jnp.zeros_like(q))
    q2 = jnp.where(lane >= DIFF_HEAD_DIM, q, jnp.zeros_like(q))
    _init_state(((m1, l1, a1), (m2, l2, a2)))

    def body(i, carry):
        off = pl.multiple_of(i * tk, tk)
        ks = k_ref[0, pl.ds(off, tk), :]
        vs = v_ref[0, pl.ds(off, tk), :]
        _online_step(q1, ks, vs, m1, l1, a1)
        _online_step(q2, ks, vs, m2, l2, a2)
        return carry

    lax.fori_loop(0, k_ref.shape[1] // tk, body, 0)

    t = lam_ref[...]
    lam = (jnp.exp(jnp.sum(t[0:1] * t[1:2], axis=1, keepdims=True))
           - jnp.exp(jnp.sum(t[2:3] * t[3:4], axis=1, keepdims=True)) + lam_init)
    o = _normalised(l1, a1) - lam * _normalised(l2, a2)
    o_ref[0] = (_rmsnorm(o, gd_ref[...]) * (1.0 - lam_init)).astype(BF16)


def _diff_attention(qkv, lam_p, g_diff, layer_idx, tq, tk):
    b, s, _ = qkv.shape
    kcol, vcol = QKV_GROUP // LANES, 2 * QKV_GROUP // LANES
    lam_init = 0.8 - 0.6 * math.exp(-0.3 * layer_idx)
    st = pltpu.VMEM((tq, LANES), F32)
    return pl.pallas_call(
        functools.partial(_diff_attn_kernel, tk=tk, lam_init=lam_init),
        out_shape=jax.ShapeDtypeStruct((b, s, DIFF_WIDTH), BF16),
        grid=(b, DIFF_HEADS, s // tq),
        in_specs=[pl.BlockSpec((4, DIFF_HEAD_DIM), lambda bi, h, qi: (0, 0)),
                  pl.BlockSpec((1, DIFF_V_DIM), lambda bi, h, qi: (0, 0)),
                  pl.BlockSpec((1, tq, LANES), lambda bi, h, qi: (bi, qi, h)),
                  pl.BlockSpec((1, s, LANES), lambda bi, h, qi: (bi, 0, kcol + h)),
                  pl.BlockSpec((1, s, LANES), lambda bi, h, qi: (bi, 0, vcol + h))],
        out_specs=pl.BlockSpec((1, tq, LANES), lambda bi, h, qi: (bi, qi, h)),
        scratch_shapes=[st] * 6,
        compiler_params=_params(("parallel", "parallel", "arbitrary")),
        name="diff_attention",
    )(lam_p, g_diff, qkv, qkv, qkv)


def _mla_attn_kernel(q_ref, k_ref, v_ref, o_ref, m1, l1, a1, m2, l2, a2, *, tk):
    qa = q_ref[0, :, :MLA_PAD_DIM]
    qb = q_ref[0, :, MLA_PAD_DIM:]
    _init_state(((m1, l1, a1), (m2, l2, a2)))

    def body(i, carry):
        off = pl.multiple_of(i * tk, tk)
        vs = v_ref[0, pl.ds(off, tk), :]
        _online_step(qa, k_ref[0, pl.ds(off, tk), :MLA_PAD_DIM], vs, m1, l1, a1)
        _online_step(qb, k_ref[0, pl.ds(off, tk), MLA_PAD_DIM:], vs, m2, l2, a2)
        return carry

    lax.fori_loop(0, k_ref.shape[1] // tk, body, 0)
    lane = lax.broadcasted_iota(jnp.int32, a1.shape, 1)
    o_ref[0] = jnp.where(lane < MLA_V_DIM, _normalised(l1, a1), _normalised(l2, a2)).astype(BF16)


def _mla_attention(q, k, v, tq, tk):
    b, s, _ = q.shape
    st = pltpu.VMEM((tq, LANES), F32)
    return pl.pallas_call(
        functools.partial(_mla_attn_kernel, tk=tk),
        out_shape=jax.ShapeDtypeStruct((b, s, MLA_WIDTH), BF16),
        grid=(b, MLA_HEADS // 2, s // tq),
        in_specs=[pl.BlockSpec((1, tq, 2 * MLA_PAD_DIM), lambda bi, h, qi: (bi, qi, h)),
                  pl.BlockSpec((1, s, 2 * MLA_PAD_DIM), lambda bi, h, qi: (bi, 0, h)),
                  pl.BlockSpec((1, s, LANES), lambda bi, h, qi: (bi, 0, h))],
        out_specs=pl.BlockSpec((1, tq, LANES), lambda bi, h, qi: (bi, qi, h)),
        scratch_shapes=[st] * 6,
        compiler_params=_params(("parallel", "parallel", "arbitrary")),
        name="mla_attention",
    )(q, k, v)


def _dil_attn_kernel(q_ref, k_ref, v_ref, o_ref, lse_ref, *, tq, win):
    sub_len = k_ref.shape[1]
    r0 = pl.program_id(2) * tq
    start = pl.multiple_of(jnp.clip(r0 - DIL_SPAN, 0, sub_len - win), DIL_SPAN)
    q = q_ref[0]
    kw = k_ref[0, pl.ds(start, win), :]
    vw = v_ref[0, pl.ds(start, win), :]
    qpos = r0 + lax.broadcasted_iota(jnp.int32, (tq, win), 0)
    kpos = start + lax.broadcasted_iota(jnp.int32, (tq, win), 1)
    band = jnp.abs(qpos - kpos) <= DIL_SPAN
    lane = lax.broadcasted_iota(jnp.int32, q.shape, 1)
    o = jnp.zeros(q.shape, F32)
    lse = jnp.zeros(q.shape, F32)
    for h in range(DIL_HEADS):
        hm = (lane // DIL_HEAD_DIM) == h
        qz = jnp.where(hm, q, jnp.zeros_like(q))
        s = lax.dot_general(qz, kw, (((1,), (1,)), ((), ())), preferred_element_type=F32)
        s = jnp.where(band, s, NEG_BIG)
        m = jnp.max(s, axis=1, keepdims=True)
        p = jnp.exp2(s - m)
        l = jnp.sum(p, axis=1, keepdims=True)
        pv = jnp.dot(p.astype(BF16), vw, preferred_element_type=F32)
        o = jnp.where(hm, pv / l, o)
        lse = jnp.where(hm, m * LN2 + jnp.log(l), lse)
    o_ref[0] = o
    lse_ref[0] = lse


def _dil_attention(qkv, group, dilation):
    b, s, _ = qkv.shape
    sub_len = s // dilation
    tq = min(128, sub_len)
    win = min(tq + 2 * DIL_SPAN, sub_len)
    nblk = 3 * QKV_GROUP // DIL_WIDTH
    qcol = DIFF_WIDTH // DIL_WIDTH + group
    kcol = qcol + QKV_GROUP // DIL_WIDTH
    vcol = kcol + QKV_GROUP // DIL_WIDTH
    sub = qkv.reshape(b, sub_len, dilation * 3 * QKV_GROUP)
    out_sd = jax.ShapeDtypeStruct((b, sub_len, dilation * DIL_WIDTH), F32)
    o, lse = pl.pallas_call(
        functools.partial(_dil_attn_kernel, tq=tq, win=win),
        out_shape=(out_sd, out_sd),
        grid=(b, dilation, sub_len // tq),
        in_specs=[pl.BlockSpec((1, tq, DIL_WIDTH), lambda bi, c, qi: (bi, qi, c * nblk + qcol)),
                  pl.BlockSpec((1, sub_len, DIL_WIDTH), lambda bi, c, qi: (bi, 0, c * nblk + kcol)),
                  pl.BlockSpec((1, sub_len, DIL_WIDTH), lambda bi, c, qi: (bi, 0, c * nblk + vcol))],
        out_specs=(pl.BlockSpec((1, tq, DIL_WIDTH), lambda bi, c, qi: (bi, qi, c)),
                   pl.BlockSpec((1, tq, DIL_WIDTH), lambda bi, c, qi: (bi, qi, c))),
        compiler_params=_params(("parallel", "parallel", "arbitrary")),
        name="dil_attention_%d" % group,
    )(sub, sub, sub)
    return o.reshape(b * s, DIL_WIDTH), lse.reshape(b * s, DIL_WIDTH)


def _merge_kernel(x_ref, oa_ref, oc_ref, o0_ref, o1_ref, o2_ref, e0_ref, e1_ref, e2_ref, gt_ref,
                  wa_ref, wb_ref, wc_ref, wo_ref, y_ref):
    e0, e1, e2 = e0_ref[...], e1_ref[...], e2_ref[...]
    top = jnp.maximum(jnp.maximum(e0, e1), e2)
    p0, p1, p2 = jnp.exp(e0 - top), jnp.exp(e1 - top), jnp.exp(e2 - top)
    ob = (p0 * o0_ref[...] + p1 * o1_ref[...] + p2 * o2_ref[...]) / (p0 + p1 + p2)
    ya = jnp.dot(oa_ref[...], wa_ref[...], preferred_element_type=F32)
    yb = jnp.dot(ob.astype(BF16), wb_ref[...], preferred_element_type=F32)
    yc = jnp.dot(oc_ref[...], wc_ref[...], preferred_element_type=F32)
    merged = (gt_ref[:, :D_MODEL].astype(F32) * ya + gt_ref[:, D_MODEL:2 * D_MODEL].astype(F32) * yb
              + gt_ref[:, 2 * D_MODEL:].astype(F32) * yc)
    y_ref[...] = x_ref[...] + jnp.dot(merged.astype(BF16), wo_ref[...], preferred_element_type=F32)


def _merge_out(x2, oa, oc, dil, gates, wa, wb, wc, wo, tm):
    t = x2.shape[0]
    row = lambda w: pl.BlockSpec((tm, w), lambda i: (i, 0))
    full = lambda a: pl.BlockSpec(a.shape, lambda i: (0, 0))
    (o0, e0), (o1, e1), (o2, e2) = dil
    return pl.pallas_call(
        _merge_kernel,
        out_shape=jax.ShapeDtypeStruct((t, D_MODEL), F32),
        grid=(t // tm,),
        in_specs=[row(D_MODEL), row(DIFF_WIDTH), row(MLA_WIDTH)] + [row(DIL_WIDTH)] * 6 + [row(3 * D_MODEL)]
                 + [full(wa), full(wb), full(wc), full(wo)],
        out_specs=row(D_MODEL),
        compiler_params=_params(("parallel",)),
        name="merge_out",
    )(x2, oa, oc, o0, o1, o2, e0, e1, e2, gates, wa, wb, wc, wo)


def _mlp_kernel(x_ref, g_ref, wu_ref, wd_ref, gf_ref, y_ref, h_ref, acc_ref, *, final_norm):
    j = pl.program_id(1)

    @pl.when(j == 0)
    def _():
        h_ref[...] = _rmsnorm(x_ref[...], g_ref[...]).astype(BF16)
        acc_ref[...] = x_ref[...]

    u = jnp.dot(h_ref[...], wu_ref[...], preferred_element_type=F32)
    u = jnp.square(jnp.maximum(u, 0.0)).astype(BF16)
    acc_ref[...] += jnp.dot(u, wd_ref[...], preferred_element_type=F32)

    @pl.when(j == pl.num_programs(1) - 1)
    def _():
        if final_norm:
            y_ref[...] = _rmsnorm(acc_ref[...], gf_ref[...])
        else:
            y_ref[...] = acc_ref[...]


def _mlp(x2, g, wu, wd, g_final, final_norm, tm, th):
    t = x2.shape[0]
    return pl.pallas_call(
        functools.partial(_mlp_kernel, final_norm=final_norm),
        out_shape=jax.ShapeDtypeStruct((t, D_MODEL), F32),
        grid=(t // tm, MLP_HIDDEN // th),
        in_specs=[pl.BlockSpec((tm, D_MODEL), lambda i, j: (i, 0)),
                  pl.BlockSpec((1, D_MODEL), lambda i, j: (0, 0)),
                  pl.BlockSpec((D_MODEL, th), lambda i, j: (0, j)),
                  pl.BlockSpec((th, D_MODEL), lambda i, j: (j, 0)),
                  pl.BlockSpec((1, D_MODEL), lambda i, j: (0, 0))],
        out_specs=pl.BlockSpec((tm, D_MODEL), lambda i, j: (i, 0)),
        scratch_shapes=[pltpu.VMEM((tm, D_MODEL), BF16), pltpu.VMEM((tm, D_MODEL), F32)],
        compiler_params=_params(("parallel", "arbitrary")),
        name="mlp",
    )(x2, g, wu, wd, g_final)


def _rope_tables(seq, dim, first_lane, period):
    half = dim // 2
    inv = ROPE_THETA ** (-jnp.arange(half, dtype=F32) / half)
    ang = jnp.arange(seq, dtype=F32)[:, None] * inv[None, :]
    lane = jnp.arange(LANES)
    rel = (lane % period) - first_lane
    active = (rel >= 0) & (rel < dim)
    idx = jnp.clip(rel, 0, dim - 1) % half
    cos = jnp.where(active[None, :], jnp.cos(ang)[:, idx], 1.0)
    sin = jnp.where(active[None, :], jnp.sin(ang)[:, idx], 0.0)
    first = (rel < half)[None, :]
    return cos, jnp.where(first, -sin, 0.0), jnp.where(first, 0.0, sin)


def _layer_weights(w_in, w_uq, w_ukv):
    def cols(a, n):
        return w_in[:, a:a + n]
    dq, dk, dv = cols(0, 512), cols(512, 512), cols(1024, 512)
    dil = [[cols(1536 + 768 * g + 256 * t, 256) for t in range(3)] for g in range(3)]
    w_qkv = jnp.concatenate([dq] + [dil[g][0] for g in range(3)] + [dk] + [dil[g][1] for g in range(3)]
                            + [dv] + [dil[g][2] for g in range(3)], axis=1).astype(BF16)
    base = 3840
    w_cq = cols(base, MLA_Q_LORA).astype(BF16)
    w_ckv = cols(base + MLA_Q_LORA, MLA_KV_LORA).astype(BF16)
    w_kr = cols(base + MLA_Q_LORA + MLA_KV_LORA, MLA_ROPE_DIM)
    w_kr = jnp.pad(w_kr, ((0, 0), (MLA_NOPE_DIM, MLA_PAD_DIM - MLA_NOPE_DIM - MLA_ROPE_DIM))).astype(BF16)
    w_gate = cols(base + MLA_Q_LORA + MLA_KV_LORA + MLA_ROPE_DIM, 3 * D_MODEL).astype(BF16)
    qd = MLA_NOPE_DIM + MLA_ROPE_DIM
    w_uq_p = jnp.pad(w_uq.reshape(MLA_Q_LORA, MLA_HEADS, qd), ((0, 0), (0, 0), (0, MLA_PAD_DIM - qd)))
    w_uq_p = w_uq_p.reshape(MLA_Q_LORA, MLA_HEADS * MLA_PAD_DIM).astype(BF16)
    kv = w_ukv.reshape(MLA_KV_LORA, MLA_HEADS, MLA_NOPE_DIM + MLA_V_DIM)
    w_uk_p = jnp.pad(kv[:, :, :MLA_NOPE_DIM], ((0, 0), (0, 0), (0, MLA_PAD_DIM - MLA_NOPE_DIM)))
    w_uk_p = w_uk_p.reshape(MLA_KV_LORA, MLA_HEADS * MLA_PAD_DIM).astype(BF16)
    w_uv = kv[:, :, MLA_NOPE_DIM:].reshape(MLA_KV_LORA, MLA_WIDTH).astype(BF16)
    return w_qkv, w_cq, w_ckv, w_kr, w_gate, w_uq_p, w_uk_p, w_uv


def kernel(x, w_in, b_gate, g_mix, diff_lambda, g_diff, g_cq, g_ckv, w_uq, w_ukv, w_o_diff, w_o_dil, w_o_mla,
           w_out, g_mlp, w_up, w_down, g_final):
    b, s, d = x.shape
    assert d == D_MODEL and s % 1024 == 0
    depth = w_in.shape[0]
    tm = min(1024, s)
    tq, tk = 256, min(512, s)
    tabs64 = _rope_tables(s, DIFF_HEAD_DIM, 0, DIFF_HEAD_DIM)
    tabs_mla = _rope_tables(s, MLA_ROPE_DIM, MLA_NOPE_DIM, MLA_PAD_DIM)
    x2 = x.reshape(b * s, d)
    for l in range(depth):
        w_qkv, w_cq, w_ckv, w_kr, w_gate, w_uq_p, w_uk_p, w_uv = _layer_weights(w_in[l], w_uq[l], w_ukv[l])
        g = g_mix[l][None, :]
        qkv = _qkv_proj(x2, g, w_qkv, tabs64, s, tm).reshape(b, s, 3 * QKV_GROUP)
        gates = _gate_proj(x2, g, w_gate, b_gate[l][:, None, :], tm)
        q_c, k_c, v_c = _mla_prologue(x2, g, w_cq, w_ckv, w_kr, g_cq[l][None, :], g_ckv[l][None, :],
                                      w_uq_p, w_uk_p, w_uv, tabs_mla, s, min(512, tm))
        o_a = _diff_attention(qkv, diff_lambda[l], g_diff[l][None, :], l, tq, tk)
        dil = [_dil_attention(qkv, gi, dilation) for gi, (_, dilation) in enumerate(DIL_PATTERNS)]
        hp = MLA_HEADS * MLA_PAD_DIM
        o_c = _mla_attention(q_c.reshape(b, s, hp), k_c.reshape(b, s, hp), v_c.reshape(b, s, MLA_WIDTH), tq, tk)
        x2 = _merge_out(x2, o_a.reshape(b * s, DIFF_WIDTH), o_c.reshape(b * s, MLA_WIDTH), dil, gates,
                        w_o_diff[l].astype(BF16), w_o_dil[l].astype(BF16), w_o_mla[l].astype(BF16),
                        w_out[l].astype(BF16), min(512, tm))
        x2 = _mlp(x2, g_mlp[l][None, :], w_up[l].astype(BF16), w_down[l].astype(BF16), g_final[None, :],
                  l == depth - 1, tm, 1024)
    return x2.reshape(b, s, d)
```

```python
import functools
import math

import jax
import jax.numpy as jnp
from jax import lax
from jax.experimental import pallas as pl
from jax.experimental.pallas import tpu as pltpu

F32 = jnp.float32
BF16 = jnp.bfloat16

D_MODEL = 1024
ROPE_THETA = 10000.0
NORM_EPS = 1e-6
NEG_BIG = -1e30
LOG2E = math.log2(math.e)
LN2 = math.log(2.0)

DIFF_HEADS = 4
DIFF_HEAD_DIM = 64
DIFF_V_DIM = 128
DIFF_WIDTH = 512

DIL_PATTERNS = ((128, 1), (512, 4), (2048, 16))
DIL_SPAN = 64
DIL_HEADS = 4
DIL_HEAD_DIM = 64
DIL_WIDTH = 256

MLA_HEADS = 8
MLA_NOPE_DIM = 64
MLA_ROPE_DIM = 32
MLA_V_DIM = 64
MLA_Q_LORA = 768
MLA_KV_LORA = 256
MLA_WIDTH = 512
MLA_PAD_DIM = 128

MLP_HIDDEN = 4096
LANES = 128

QKV_GROUP = 1280
VMEM_LIMIT = 56 * 1024 * 1024


def _params(semantics):
    return pltpu.CompilerParams(dimension_semantics=semantics, vmem_limit_bytes=VMEM_LIMIT)


def _rmsnorm(x, g):
    ms = jnp.mean(x * x, axis=-1, keepdims=True)
    return x * lax.rsqrt(ms + NORM_EPS) * g


def _rope_chunk(a, c, sa, sb, half):
    return a * c + pltpu.roll(a, LANES - half, 1) * sa + pltpu.roll(a, half, 1) * sb


def _qkv_kernel(x_ref, g_ref, w_ref, c_ref, sa_ref, sb_ref, o_ref, h_ref, *, qscale):
    j = pl.program_id(1)

    @pl.when(j == 0)
    def _():
        h_ref[...] = _rmsnorm(x_ref[...], g_ref[...]).astype(BF16)

    acc = jnp.dot(h_ref[...], w_ref[...], preferred_element_type=F32)

    @pl.when(j == 2)
    def _():
        o_ref[...] = acc.astype(BF16)

    @pl.when(j < 2)
    def _():
        scale = jnp.where(j == 0, qscale, 1.0).astype(F32)
        c = c_ref[...] * scale
        sa = sa_ref[...] * scale
        sb = sb_ref[...] * scale
        for ch in range(QKV_GROUP // LANES):
            sl = slice(ch * LANES, (ch + 1) * LANES)
            o_ref[:, sl] = _rope_chunk(acc[:, sl], c, sa, sb, DIFF_HEAD_DIM // 2).astype(BF16)


def _qkv_proj(x2, g, w, tabs, seq, tm):
    t = x2.shape[0]
    nrow = seq // tm
    tab_spec = pl.BlockSpec((tm, LANES), lambda i, j: (i % nrow, 0))
    return pl.pallas_call(
        functools.partial(_qkv_kernel, qscale=DIFF_HEAD_DIM ** -0.5 * LOG2E),
        out_shape=jax.ShapeDtypeStruct((t, 3 * QKV_GROUP), BF16),
        grid=(t // tm, 3),
        in_specs=[pl.BlockSpec((tm, D_MODEL), lambda i, j: (i, 0)),
                  pl.BlockSpec((1, D_MODEL), lambda i, j: (0, 0)),
                  pl.BlockSpec((D_MODEL, QKV_GROUP), lambda i, j: (0, j)),
                  tab_spec, tab_spec, tab_spec],
        out_specs=pl.BlockSpec((tm, QKV_GROUP), lambda i, j: (i, j)),
        scratch_shapes=[pltpu.VMEM((tm, D_MODEL), BF16)],
        compiler_params=_params(("parallel", "arbitrary")),
        name="qkv_proj",
    )(x2, g, w, *tabs)


def _gate_kernel(x_ref, g_ref, w_ref, b_ref, o_ref, h_ref):
    @pl.when(pl.program_id(1) == 0)
    def _():
        h_ref[...] = _rmsnorm(x_ref[...], g_ref[...]).astype(BF16)

    z = jnp.dot(h_ref[...], w_ref[...], preferred_element_type=F32) + b_ref[0]
    o_ref[...] = (1.0 / (1.0 + jnp.exp(-z))).astype(BF16)


def _gate_proj(x2, g, w, b, tm):
    t = x2.shape[0]
    return pl.pallas_call(
        _gate_kernel,
        out_shape=jax.ShapeDtypeStruct((t, 3 * D_MODEL), BF16),
        grid=(t // tm, 3),
        in_specs=[pl.BlockSpec((tm, D_MODEL), lambda i, j: (i, 0)),
                  pl.BlockSpec((1, D_MODEL), lambda i, j: (0, 0)),
                  pl.BlockSpec((D_MODEL, D_MODEL), lambda i, j: (0, j)),
                  pl.BlockSpec((1, 1, D_MODEL), lambda i, j: (j, 0, 0))],
        out_specs=pl.BlockSpec((tm, D_MODEL), lambda i, j: (i, j)),
        scratch_shapes=[pltpu.VMEM((tm, D_MODEL), BF16)],
        compiler_params=_params(("parallel", "arbitrary")),
        name="gate_proj",
    )(x2, g, w, b)


def _mla_pro_kernel(x_ref, g_ref, wcq_ref, wckv_ref, wkr_ref, gcq_ref, gckv_ref, wuq_ref, wuk_ref, wuv_ref,
                    c_ref, sa_ref, sb_ref, q_ref, k_ref, v_ref, *, qscale):
    h = _rmsnorm(x_ref[...], g_ref[...]).astype(BF16)
    cq = jnp.dot(h, wcq_ref[...], preferred_element_type=F32)
    cq = _rmsnorm(cq, gcq_ref[...]).astype(BF16)
    ckv = jnp.dot(h, wckv_ref[...], preferred_element_type=F32)
    ckv = _rmsnorm(ckv, gckv_ref[...]).astype(BF16)
    c, sa, sb = c_ref[...], sa_ref[...], sb_ref[...]
    half = MLA_ROPE_DIM // 2
    kr = _rope_chunk(jnp.dot(h, wkr_ref[...], preferred_element_type=F32), c, sa, sb, half)
    qh = jnp.dot(cq, wuq_ref[...], preferred_element_type=F32)
    kn = jnp.dot(ckv, wuk_ref[...], preferred_element_type=F32)
    cq_s, sa_s, sb_s = c * qscale, sa * qscale, sb * qscale
    for hd in range(MLA_HEADS):
        sl = slice(hd * MLA_PAD_DIM, (hd + 1) * MLA_PAD_DIM)
        q_ref[:, sl] = _rope_chunk(qh[:, sl], cq_s, sa_s, sb_s, half).astype(BF16)
        k_ref[:, sl] = (kn[:, sl] + kr).astype(BF16)
    v_ref[...] = jnp.dot(ckv, wuv_ref[...], preferred_element_type=F32).astype(BF16)


def _mla_prologue(x2, g, wcq, wckv, wkr, gcq, gckv, wuq, wuk, wuv, tabs, seq, tm):
    t = x2.shape[0]
    nrow = seq // tm
    full = lambda a: pl.BlockSpec(a.shape, lambda i: (0,) * a.ndim)
    tab_spec = pl.BlockSpec((tm, LANES), lambda i: (i % nrow, 0))
    hp = MLA_HEADS * MLA_PAD_DIM
    return pl.pallas_call(
        functools.partial(_mla_pro_kernel, qscale=(MLA_NOPE_DIM + MLA_ROPE_DIM) ** -0.5 * LOG2E),
        out_shape=(jax.ShapeDtypeStruct((t, hp), BF16), jax.ShapeDtypeStruct((t, hp), BF16),
                   jax.ShapeDtypeStruct((t, MLA_WIDTH), BF16)),
        grid=(t // tm,),
        in_specs=[pl.BlockSpec((tm, D_MODEL), lambda i: (i, 0)), full(g), full(wcq), full(wckv), full(wkr),
                  full(gcq), full(gckv), full(wuq), full(wuk), full(wuv), tab_spec, tab_spec, tab_spec],
        out_specs=(pl.BlockSpec((tm, hp), lambda i: (i, 0)), pl.BlockSpec((tm, hp), lambda i: (i, 0)),
                   pl.BlockSpec((tm, MLA_WIDTH), lambda i: (i, 0))),
        compiler_params=_params(("parallel",)),
        name="mla_prologue",
    )(x2, g, wcq, wckv, wkr, gcq, gckv, wuq, wuk, wuv, *tabs)


_NT = (((1,), (1,)), ((), ()))
N_STREAMS = 2


def _chunk(off, tk):
    return pl.ds(off if isinstance(off, int) else pl.multiple_of(off, tk), tk)


def _softmax_phase(st, slot):
    sbuf, pbuf, abuf, m_ref, l_ref, _ = st
    for z in range(N_STREAMS):
        s = sbuf[z, slot]
        tk = s.shape[1]
        m_prev = m_ref[z]
        m_new = jnp.maximum(m_prev, jnp.max(s, axis=1, keepdims=True))
        alpha = jnp.exp2(m_prev - m_new)
        p = jnp.exp2(s - pltpu.repeat(m_new, tk // LANES, axis=1))
        psum = p[:, :LANES]
        for c in range(1, tk // LANES):
            psum = psum + p[:, c * LANES:(c + 1) * LANES]
        m_ref[z] = m_new
        l_ref[z] = alpha * l_ref[z] + psum
        abuf[z, slot] = alpha
        pbuf[z, slot] = p.astype(BF16)


def _pv_phase(st, v, slot):
    _, pbuf, abuf, _, _, a_ref = st
    for z in range(N_STREAMS):
        a_ref[z] = abuf[z, slot] * a_ref[z] + jnp.dot(pbuf[z, slot], v, preferred_element_type=F32)


def _attend(qk_phase, v_of, nk, tk, st):
    assert nk % 2 == 0 and nk >= 2
    sbuf, pbuf, abuf, m_ref, l_ref, a_ref = st
    m_ref[...] = jnp.full(m_ref.shape, NEG_BIG, F32)
    l_ref[...] = jnp.zeros(l_ref.shape, F32)
    a_ref[...] = jnp.zeros(a_ref.shape, F32)
    qk_phase(0, 0)
    qk_phase(tk, 1)
    _softmax_phase(st, 0)

    def trip(j, carry):
        off = pl.multiple_of(2 * j * tk, 2 * tk)
        qk_phase(off + 2 * tk, 0)
        _pv_phase(st, v_of(off), 0)
        _softmax_phase(st, 1)
        qk_phase(off + 3 * tk, 1)
        _pv_phase(st, v_of(off + tk), 1)
        _softmax_phase(st, 0)
        return carry

    lax.fori_loop(0, (nk - 2) // 2, trip, 0)
    _pv_phase(st, v_of((nk - 2) * tk), 0)
    _softmax_phase(st, 1)
    _pv_phase(st, v_of((nk - 1) * tk), 1)


def _normalised(st, z):
    return st[5][z] / jnp.sum(st[4][z], axis=1, keepdims=True)


def _attn_scratch(tq, tk):
    return [pltpu.VMEM((N_STREAMS, 2, tq, tk), F32), pltpu.VMEM((N_STREAMS, 2, tq, tk), BF16),
            pltpu.VMEM((N_STREAMS, 2, tq, LANES), F32)] + [pltpu.VMEM((N_STREAMS, tq, LANES), F32)] * 3


def _diff_attn_kernel(lam_ref, gd_ref, q_ref, k_ref, v_ref, o_ref, *st, tk, lam_init):
    q = q_ref[0]
    tq = q.shape[0]
    lane = lax.broadcasted_iota(jnp.int32, q.shape, 1)
    zero = jnp.zeros_like(q)
    q12 = jnp.concatenate([jnp.where(lane < DIFF_HEAD_DIM, q, zero),
                           jnp.where(lane >= DIFF_HEAD_DIM, q, zero)], axis=0)

    def qk_phase(off, slot):
        s = lax.dot_general(q12, k_ref[0, _chunk(off, tk), :], _NT, preferred_element_type=F32)
        st[0][0, slot] = s[:tq]
        st[0][1, slot] = s[tq:]

    _attend(qk_phase, lambda off: v_ref[0, _chunk(off, tk), :], k_ref.shape[1] // tk, tk, st)

    t = lam_ref[...]
    lam = (jnp.exp(jnp.sum(t[0:1] * t[1:2], axis=1, keepdims=True))
           - jnp.exp(jnp.sum(t[2:3] * t[3:4], axis=1, keepdims=True)) + lam_init)
    o = _normalised(st, 0) - lam * _normalised(st, 1)
    o_ref[0] = (_rmsnorm(o, gd_ref[...]) * (1.0 - lam_init)).astype(BF16)


def _diff_attention(qkv, lam_p, g_diff, layer_idx, tq, tk):
    b, s, _ = qkv.shape
    kcol, vcol = QKV_GROUP // LANES, 2 * QKV_GROUP // LANES
    lam_init = 0.8 - 0.6 * math.exp(-0.3 * layer_idx)
    return pl.pallas_call(
        functools.partial(_diff_attn_kernel, tk=tk, lam_init=lam_init),
        out_shape=jax.ShapeDtypeStruct((b, s, DIFF_WIDTH), BF16),
        grid=(b, DIFF_HEADS, s // tq),
        in_specs=[pl.BlockSpec((4, DIFF_HEAD_DIM), lambda bi, h, qi: (0, 0)),
                  pl.BlockSpec((1, DIFF_V_DIM), lambda bi, h, qi: (0, 0)),
                  pl.BlockSpec((1, tq, LANES), lambda bi, h, qi: (bi, qi, h)),
                  pl.BlockSpec((1, s, LANES), lambda bi, h, qi: (bi, 0, kcol + h)),
                  pl.BlockSpec((1, s, LANES), lambda bi, h, qi: (bi, 0, vcol + h))],
        out_specs=pl.BlockSpec((1, tq, LANES), lambda bi, h, qi: (bi, qi, h)),
        scratch_shapes=_attn_scratch(tq, tk),
        compiler_params=_params(("parallel", "parallel", "arbitrary")),
        name="diff_attention",
    )(lam_p, g_diff, qkv, qkv, qkv)


def _mla_attn_kernel(q_ref, k_ref, v_ref, o_ref, *st, tk):
    qs = (q_ref[0, :, :MLA_PAD_DIM], q_ref[0, :, MLA_PAD_DIM:])

    def qk_phase(off, slot):
        for z in range(N_STREAMS):
            kz = k_ref[0, _chunk(off, tk), z * MLA_PAD_DIM:(z + 1) * MLA_PAD_DIM]
            st[0][z, slot] = lax.dot_general(qs[z], kz, _NT, preferred_element_type=F32)

    _attend(qk_phase, lambda off: v_ref[0, _chunk(off, tk), :], k_ref.shape[1] // tk, tk, st)
    lane = lax.broadcasted_iota(jnp.int32, o_ref.shape[1:], 1)
    o_ref[0] = jnp.where(lane < MLA_V_DIM, _normalised(st, 0), _normalised(st, 1)).astype(BF16)


def _mla_attention(q, k, v, tq, tk):
    b, s, _ = q.shape
    return pl.pallas_call(
        functools.partial(_mla_attn_kernel, tk=tk),
        out_shape=jax.ShapeDtypeStruct((b, s, MLA_WIDTH), BF16),
        grid=(b, MLA_HEADS // 2, s // tq),
        in_specs=[pl.BlockSpec((1, tq, 2 * MLA_PAD_DIM), lambda bi, h, qi: (bi, qi, h)),
                  pl.BlockSpec((1, s, 2 * MLA_PAD_DIM), lambda bi, h, qi: (bi, 0, h)),
                  pl.BlockSpec((1, s, LANES), lambda bi, h, qi: (bi, 0, h))],
        out_specs=pl.BlockSpec((1, tq, LANES), lambda bi, h, qi: (bi, qi, h)),
        scratch_shapes=_attn_scratch(tq, tk),
        compiler_params=_params(("parallel", "parallel", "arbitrary")),
        name="mla_attention",
    )(q, k, v)


def _dil_attn_kernel(q_ref, k_ref, v_ref, o_ref, lse_ref, *, tq, win):
    sub_len = k_ref.shape[1]
    r0 = pl.program_id(2) * tq
    start = pl.multiple_of(jnp.clip(r0 - DIL_SPAN, 0, sub_len - win), DIL_SPAN)
    q = q_ref[0]
    kw = k_ref[0, pl.ds(start, win), :]
    vw = v_ref[0, pl.ds(start, win), :]
    qpos = r0 + lax.broadcasted_iota(jnp.int32, (tq, win), 0)
    kpos = start + lax.broadcasted_iota(jnp.int32, (tq, win), 1)
    band = jnp.abs(qpos - kpos) <= DIL_SPAN
    lane = lax.broadcasted_iota(jnp.int32, q.shape, 1)
    o = jnp.zeros(q.shape, F32)
    lse = jnp.zeros(q.shape, F32)
    for h in range(DIL_HEADS):
        hm = (lane // DIL_HEAD_DIM) == h
        qz = jnp.where(hm, q, jnp.zeros_like(q))
        s = lax.dot_general(qz, kw, (((1,), (1,)), ((), ())), preferred_element_type=F32)
        s = jnp.where(band, s, NEG_BIG)
        m = jnp.max(s, axis=1, keepdims=True)
        p = jnp.exp2(s - m)
        l = jnp.sum(p, axis=1, keepdims=True)
        pv = jnp.dot(p.astype(BF16), vw, preferred_element_type=F32)
        o = jnp.where(hm, pv / l, o)
        lse = jnp.where(hm, m * LN2 + jnp.log(l), lse)
    o_ref[0] = o
    lse_ref[0] = lse


def _dil_attention(qkv, group, dilation):
    b, s, _ = qkv.shape
    sub_len = s // dilation
    tq = min(128, sub_len)
    win = min(tq + 2 * DIL_SPAN, sub_len)
    nblk = 3 * QKV_GROUP // DIL_WIDTH
    qcol = DIFF_WIDTH // DIL_WIDTH + group
    kcol = qcol + QKV_GROUP // DIL_WIDTH
    vcol = kcol + QKV_GROUP // DIL_WIDTH
    sub = qkv.reshape(b, sub_len, dilation * 3 * QKV_GROUP)
    out_sd = jax.ShapeDtypeStruct((b, sub_len, dilation * DIL_WIDTH), F32)
    o, lse = pl.pallas_call(
        functools.partial(_dil_attn_kernel, tq=tq, win=win),
        out_shape=(out_sd, out_sd),
        grid=(b, dilation, sub_len // tq),
        in_specs=[pl.BlockSpec((1, tq, DIL_WIDTH), lambda bi, c, qi: (bi, qi, c * nblk + qcol)),
                  pl.BlockSpec((1, sub_len, DIL_WIDTH), lambda bi, c, qi: (bi, 0, c * nblk + kcol)),
                  pl.BlockSpec((1, sub_len, DIL_WIDTH), lambda bi, c, qi: (bi, 0, c * nblk + vcol))],
        out_specs=(pl.BlockSpec((1, tq, DIL_WIDTH), lambda bi, c, qi: (bi, qi, c)),
                   pl.BlockSpec((1, tq, DIL_WIDTH), lambda bi, c, qi: (bi, qi, c))),
        compiler_params=_params(("parallel", "parallel", "arbitrary")),
        name="dil_attention_%d" % group,
    )(sub, sub, sub)
    return o.reshape(b * s, DIL_WIDTH), lse.reshape(b * s, DIL_WIDTH)


def _merge_kernel(x_ref, oa_ref, oc_ref, o0_ref, o1_ref, o2_ref, e0_ref, e1_ref, e2_ref, gt_ref,
                  wa_ref, wb_ref, wc_ref, wo_ref, y_ref):
    e0, e1, e2 = e0_ref[...], e1_ref[...], e2_ref[...]
    top = jnp.maximum(jnp.maximum(e0, e1), e2)
    p0, p1, p2 = jnp.exp(e0 - top), jnp.exp(e1 - top), jnp.exp(e2 - top)
    ob = (p0 * o0_ref[...] + p1 * o1_ref[...] + p2 * o2_ref[...]) / (p0 + p1 + p2)
    ya = jnp.dot(oa_ref[...], wa_ref[...], preferred_element_type=F32)
    yb = jnp.dot(ob.astype(BF16), wb_ref[...], preferred_element_type=F32)
    yc = jnp.dot(oc_ref[...], wc_ref[...], preferred_element_type=F32)
    merged = (gt_ref[:, :D_MODEL].astype(F32) * ya + gt_ref[:, D_MODEL:2 * D_MODEL].astype(F32) * yb
              + gt_ref[:, 2 * D_MODEL:].astype(F32) * yc)
    y_ref[...] = x_ref[...] + jnp.dot(merged.astype(BF16), wo_ref[...], preferred_element_type=F32)


def _merge_out(x2, oa, oc, dil, gates, wa, wb, wc, wo, tm):
    t = x2.shape[0]
    row = lambda w: pl.BlockSpec((tm, w), lambda i: (i, 0))
    full = lambda a: pl.BlockSpec(a.shape, lambda i: (0, 0))
    (o0, e0), (o1, e1), (o2, e2) = dil
    return pl.pallas_call(
        _merge_kernel,
        out_shape=jax.ShapeDtypeStruct((t, D_MODEL), F32),
        grid=(t // tm,),
        in_specs=[row(D_MODEL), row(DIFF_WIDTH), row(MLA_WIDTH)] + [row(DIL_WIDTH)] * 6 + [row(3 * D_MODEL)]
                 + [full(wa), full(wb), full(wc), full(wo)],
        out_specs=row(D_MODEL),
        compiler_params=_params(("parallel",)),
        name="merge_out",
    )(x2, oa, oc, o0, o1, o2, e0, e1, e2, gates, wa, wb, wc, wo)


def _mlp_kernel(x_ref, g_ref, wu_ref, wd_ref, gf_ref, y_ref, h_ref, acc_ref, *, final_norm):
    j = pl.program_id(1)

    @pl.when(j == 0)
    def _():
        h_ref[...] = _rmsnorm(x_ref[...], g_ref[...]).astype(BF16)
        acc_ref[...] = x_ref[...]

    u = jnp.dot(h_ref[...], wu_ref[...], preferred_element_type=F32)
    u = jnp.square(jnp.maximum(u, 0.0)).astype(BF16)
    acc_ref[...] += jnp.dot(u, wd_ref[...], preferred_element_type=F32)

    @pl.when(j == pl.num_programs(1) - 1)
    def _():
        if final_norm:
            y_ref[...] = _rmsnorm(acc_ref[...], gf_ref[...])
        else:
            y_ref[...] = acc_ref[...]


def _mlp(x2, g, wu, wd, g_final, final_norm, tm, th):
    t = x2.shape[0]
    return pl.pallas_call(
        functools.partial(_mlp_kernel, final_norm=final_norm),
        out_shape=jax.ShapeDtypeStruct((t, D_MODEL), F32),
        grid=(t // tm, MLP_HIDDEN // th),
        in_specs=[pl.BlockSpec((tm, D_MODEL), lambda i, j: (i, 0)),
                  pl.BlockSpec((1, D_MODEL), lambda i, j: (0, 0)),
                  pl.BlockSpec((D_MODEL, th), lambda i, j: (0, j)),
                  pl.BlockSpec((th, D_MODEL), lambda i, j: (j, 0)),
                  pl.BlockSpec((1, D_MODEL), lambda i, j: (0, 0))],
        out_specs=pl.BlockSpec((tm, D_MODEL), lambda i, j: (i, 0)),
        scratch_shapes=[pltpu.VMEM((tm, D_MODEL), BF16), pltpu.VMEM((tm, D_MODEL), F32)],
        compiler_params=_params(("parallel", "arbitrary")),
        name="mlp",
    )(x2, g, wu, wd, g_final)


def _rope_tables(seq, dim, first_lane, period):
    half = dim // 2
    inv = ROPE_THETA ** (-jnp.arange(half, dtype=F32) / half)
    ang = jnp.arange(seq, dtype=F32)[:, None] * inv[None, :]
    lane = jnp.arange(LANES)
    rel = (lane % period) - first_lane
    active = (rel >= 0) & (rel < dim)
    idx = jnp.clip(rel, 0, dim - 1) % half
    cos = jnp.where(active[None, :], jnp.cos(ang)[:, idx], 1.0)
    sin = jnp.where(active[None, :], jnp.sin(ang)[:, idx], 0.0)
    first = (rel < half)[None, :]
    return cos, jnp.where(first, -sin, 0.0), jnp.where(first, 0.0, sin)


def _layer_weights(w_in, w_uq, w_ukv):
    def cols(a, n):
        return w_in[:, a:a + n]
    dq, dk, dv = cols(0, 512), cols(512, 512), cols(1024, 512)
    dil = [[cols(1536 + 768 * g + 256 * t, 256) for t in range(3)] for g in range(3)]
    w_qkv = jnp.concatenate([dq] + [dil[g][0] for g in range(3)] + [dk] + [dil[g][1] for g in range(3)]
                            + [dv] + [dil[g][2] for g in range(3)], axis=1).astype(BF16)
    base = 3840
    w_cq = cols(base, MLA_Q_LORA).astype(BF16)
    w_ckv = cols(base + MLA_Q_LORA, MLA_KV_LORA).astype(BF16)
    w_kr = cols(base + MLA_Q_LORA + MLA_KV_LORA, MLA_ROPE_DIM)
    w_kr = jnp.pad(w_kr, ((0, 0), (MLA_NOPE_DIM, MLA_PAD_DIM - MLA_NOPE_DIM - MLA_ROPE_DIM))).astype(BF16)
    w_gate = cols(base + MLA_Q_LORA + MLA_KV_LORA + MLA_ROPE_DIM, 3 * D_MODEL).astype(BF16)
    qd = MLA_NOPE_DIM + MLA_ROPE_DIM
    w_uq_p = jnp.pad(w_uq.reshape(MLA_Q_LORA, MLA_HEADS, qd), ((0, 0), (0, 0), (0, MLA_PAD_DIM - qd)))
    w_uq_p = w_uq_p.reshape(MLA_Q_LORA, MLA_HEADS * MLA_PAD_DIM).astype(BF16)
    kv = w_ukv.reshape(MLA_KV_LORA, MLA_HEADS, MLA_NOPE_DIM + MLA_V_DIM)
    w_uk_p = jnp.pad(kv[:, :, :MLA_NOPE_DIM], ((0, 0), (0, 0), (0, MLA_PAD_DIM - MLA_NOPE_DIM)))
    w_uk_p = w_uk_p.reshape(MLA_KV_LORA, MLA_HEADS * MLA_PAD_DIM).astype(BF16)
    w_uv = kv[:, :, MLA_NOPE_DIM:].reshape(MLA_KV_LORA, MLA_WIDTH).astype(BF16)
    return w_qkv, w_cq, w_ckv, w_kr, w_gate, w_uq_p, w_uk_p, w_uv


def kernel(x, w_in, b_gate, g_mix, diff_lambda, g_diff, g_cq, g_ckv, w_uq, w_ukv, w_o_diff, w_o_dil, w_o_mla,
           w_out, g_mlp, w_up, w_down, g_final):
    b, s, d = x.shape
    assert d == D_MODEL and s % 1024 == 0
    depth = w_in.shape[0]
    tm = min(1024, s)
    tq, tk = 256, min(512, s)
    tabs64 = _rope_tables(s, DIFF_HEAD_DIM, 0, DIFF_HEAD_DIM)
    tabs_mla = _rope_tables(s, MLA_ROPE_DIM, MLA_NOPE_DIM, MLA_PAD_DIM)
    x2 = x.reshape(b * s, d)
    for l in range(depth):
        w_qkv, w_cq, w_ckv, w_kr, w_gate, w_uq_p, w_uk_p, w_uv = _layer_weights(w_in[l], w_uq[l], w_ukv[l])
        g = g_mix[l][None, :]
        qkv = _qkv_proj(x2, g, w_qkv, tabs64, s, tm).reshape(b, s, 3 * QKV_GROUP)
        gates = _gate_proj(x2, g, w_gate, b_gate[l][:, None, :], tm)
        q_c, k_c, v_c = _mla_prologue(x2, g, w_cq, w_ckv, w_kr, g_cq[l][None, :], g_ckv[l][None, :],
                                      w_uq_p, w_uk_p, w_uv, tabs_mla, s, min(512, tm))
        o_a = _diff_attention(qkv, diff_lambda[l], g_diff[l][None, :], l, tq, tk)
        dil = [_dil_attention(qkv, gi, dilation) for gi, (_, dilation) in enumerate(DIL_PATTERNS)]
        hp = MLA_HEADS * MLA_PAD_DIM
        o_c = _mla_attention(q_c.reshape(b, s, hp), k_c.reshape(b, s, hp), v_c.reshape(b, s, MLA_WIDTH), tq, tk)
        x2 = _merge_out(x2, o_a.reshape(b * s, DIFF_WIDTH), o_c.reshape(b * s, MLA_WIDTH), dil, gates,
                        w_o_diff[l].astype(BF16), w_o_dil[l].astype(BF16), w_o_mla[l].astype(BF16),
                        w_out[l].astype(BF16), min(512, tm))
        x2 = _mlp(x2, g_mlp[l][None, :], w_up[l].astype(BF16), w_down[l].astype(BF16), g_final[None, :],
                  l == depth - 1, tm, 1024)
    return x2.reshape(b, s, d)
```

```python
import functools
import math

import jax
import jax.numpy as jnp
from jax import lax
from jax.experimental import pallas as pl
from jax.experimental.pallas import tpu as pltpu

F32 = jnp.float32
BF16 = jnp.bfloat16

D_MODEL = 1024
ROPE_THETA = 10000.0
NORM_EPS = 1e-6
NEG_BIG = -1e30
LOG2E = math.log2(math.e)
LN2 = math.log(2.0)

DIFF_HEADS = 4
DIFF_HEAD_DIM = 64
DIFF_V_DIM = 128
DIFF_WIDTH = 512

DIL_PATTERNS = ((128, 1), (512, 4), (2048, 16))
DIL_SPAN = 64
DIL_HEADS = 4
DIL_HEAD_DIM = 64
DIL_WIDTH = 256

MLA_HEADS = 8
MLA_NOPE_DIM = 64
MLA_ROPE_DIM = 32
MLA_V_DIM = 64
MLA_Q_LORA = 768
MLA_KV_LORA = 256
MLA_WIDTH = 512
MLA_PAD_DIM = 128

MLP_HIDDEN = 4096
LANES = 128

QKV_GROUP = 1280
VMEM_LIMIT = 56 * 1024 * 1024


def _params(semantics):
    return pltpu.CompilerParams(dimension_semantics=semantics, vmem_limit_bytes=VMEM_LIMIT)


def _rmsnorm(x, g):
    ms = jnp.mean(x * x, axis=-1, keepdims=True)
    return x * lax.rsqrt(ms + NORM_EPS) * g


def _rope_chunk(a, c, sa, sb, half):
    return a * c + pltpu.roll(a, LANES - half, 1) * sa + pltpu.roll(a, half, 1) * sb


def _qkv_kernel(x_ref, g_ref, w_ref, c_ref, sa_ref, sb_ref, od_ref, o0_ref, o1_ref, o2_ref, h_ref, *, qscale):
    j = pl.program_id(1)

    @pl.when(j == 0)
    def _():
        h_ref[...] = _rmsnorm(x_ref[...], g_ref[...]).astype(BF16)

    acc = jnp.dot(h_ref[...], w_ref[...], preferred_element_type=F32)

    def emit(fn):
        for ch in range(QKV_GROUP // LANES):
            val = fn(acc[:, ch * LANES:(ch + 1) * LANES]).astype(BF16)
            if ch < DIFF_WIDTH // LANES:
                od_ref[:, ch * LANES:(ch + 1) * LANES] = val
            else:
                r = ch - DIFF_WIDTH // LANES
                dst = (o0_ref, o1_ref, o2_ref)[r // 2]
                dst[:, (r % 2) * LANES:(r % 2 + 1) * LANES] = val

    @pl.when(j == 2)
    def _():
        emit(lambda a: a)

    @pl.when(j < 2)
    def _():
        scale = jnp.where(j == 0, qscale, 1.0).astype(F32)
        c = c_ref[...] * scale
        sa = sa_ref[...] * scale
        sb = sb_ref[...] * scale
        emit(lambda a: _rope_chunk(a, c, sa, sb, DIFF_HEAD_DIM // 2))


def _qkv_proj(x2, g, w, tabs, seq, tm):
    t = x2.shape[0]
    nrow = seq // tm
    tab_spec = pl.BlockSpec((tm, LANES), lambda i, j: (i % nrow, 0))
    dil_sd = jax.ShapeDtypeStruct((t, 3 * DIL_WIDTH), BF16)
    dil_spec = pl.BlockSpec((tm, DIL_WIDTH), lambda i, j: (i, j))
    return pl.pallas_call(
        functools.partial(_qkv_kernel, qscale=DIFF_HEAD_DIM ** -0.5 * LOG2E),
        out_shape=(jax.ShapeDtypeStruct((t, 3 * DIFF_WIDTH), BF16), dil_sd, dil_sd, dil_sd),
        grid=(t // tm, 3),
        in_specs=[pl.BlockSpec((tm, D_MODEL), lambda i, j: (i, 0)),
                  pl.BlockSpec((1, D_MODEL), lambda i, j: (0, 0)),
                  pl.BlockSpec((D_MODEL, QKV_GROUP), lambda i, j: (0, j)),
                  tab_spec, tab_spec, tab_spec],
        out_specs=(pl.BlockSpec((tm, DIFF_WIDTH), lambda i, j: (i, j)), dil_spec, dil_spec, dil_spec),
        scratch_shapes=[pltpu.VMEM((tm, D_MODEL), BF16)],
        compiler_params=_params(("parallel", "arbitrary")),
        name="qkv_proj",
    )(x2, g, w, *tabs)


def _gate_kernel(x_ref, g_ref, w_ref, b_ref, o_ref, h_ref):
    @pl.when(pl.program_id(1) == 0)
    def _():
        h_ref[...] = _rmsnorm(x_ref[...], g_ref[...]).astype(BF16)

    z = jnp.dot(h_ref[...], w_ref[...], preferred_element_type=F32) + b_ref[0]
    o_ref[...] = (1.0 / (1.0 + jnp.exp(-z))).astype(BF16)


def _gate_proj(x2, g, w, b, tm):
    t = x2.shape[0]
    return pl.pallas_call(
        _gate_kernel,
        out_shape=jax.ShapeDtypeStruct((t, 3 * D_MODEL), BF16),
        grid=(t // tm, 3),
        in_specs=[pl.BlockSpec((tm, D_MODEL), lambda i, j: (i, 0)),
                  pl.BlockSpec((1, D_MODEL), lambda i, j: (0, 0)),
                  pl.BlockSpec((D_MODEL, D_MODEL), lambda i, j: (0, j)),
                  pl.BlockSpec((1, 1, D_MODEL), lambda i, j: (j, 0, 0))],
        out_specs=pl.BlockSpec((tm, D_MODEL), lambda i, j: (i, j)),
        scratch_shapes=[pltpu.VMEM((tm, D_MODEL), BF16)],
        compiler_params=_params(("parallel", "arbitrary")),
        name="gate_proj",
    )(x2, g, w, b)


def _mla_pro_kernel(x_ref, g_ref, wcq_ref, wckv_ref, wkr_ref, gcq_ref, gckv_ref, wuq_ref, wuk_ref, wuv_ref,
                    c_ref, sa_ref, sb_ref, q_ref, k_ref, v_ref, *, qscale):
    h = _rmsnorm(x_ref[...], g_ref[...]).astype(BF16)
    cq = jnp.dot(h, wcq_ref[...], preferred_element_type=F32)
    cq = _rmsnorm(cq, gcq_ref[...]).astype(BF16)
    ckv = jnp.dot(h, wckv_ref[...], preferred_element_type=F32)
    ckv = _rmsnorm(ckv, gckv_ref[...]).astype(BF16)
    c, sa, sb = c_ref[...], sa_ref[...], sb_ref[...]
    half = MLA_ROPE_DIM // 2
    kr = _rope_chunk(jnp.dot(h, wkr_ref[...], preferred_element_type=F32), c, sa, sb, half)
    qh = jnp.dot(cq, wuq_ref[...], preferred_element_type=F32)
    kn = jnp.dot(ckv, wuk_ref[...], preferred_element_type=F32)
    cq_s, sa_s, sb_s = c * qscale, sa * qscale, sb * qscale
    for hd in range(MLA_HEADS):
        sl = slice(hd * MLA_PAD_DIM, (hd + 1) * MLA_PAD_DIM)
        q_ref[:, sl] = _rope_chunk(qh[:, sl], cq_s, sa_s, sb_s, half).astype(BF16)
        k_ref[:, sl] = (kn[:, sl] + kr).astype(BF16)
    v_ref[...] = jnp.dot(ckv, wuv_ref[...], preferred_element_type=F32).astype(BF16)


def _mla_prologue(x2, g, wcq, wckv, wkr, gcq, gckv, wuq, wuk, wuv, tabs, seq, tm):
    t = x2.shape[0]
    nrow = seq // tm
    full = lambda a: pl.BlockSpec(a.shape, lambda i: (0,) * a.ndim)
    tab_spec = pl.BlockSpec((tm, LANES), lambda i: (i % nrow, 0))
    hp = MLA_HEADS * MLA_PAD_DIM
    return pl.pallas_call(
        functools.partial(_mla_pro_kernel, qscale=(MLA_NOPE_DIM + MLA_ROPE_DIM) ** -0.5 * LOG2E),
        out_shape=(jax.ShapeDtypeStruct((t, hp), BF16), jax.ShapeDtypeStruct((t, hp), BF16),
                   jax.ShapeDtypeStruct((t, MLA_WIDTH), BF16)),
        grid=(t // tm,),
        in_specs=[pl.BlockSpec((tm, D_MODEL), lambda i: (i, 0)), full(g), full(wcq), full(wckv), full(wkr),
                  full(gcq), full(gckv), full(wuq), full(wuk), full(wuv), tab_spec, tab_spec, tab_spec],
        out_specs=(pl.BlockSpec((tm, hp), lambda i: (i, 0)), pl.BlockSpec((tm, hp), lambda i: (i, 0)),
                   pl.BlockSpec((tm, MLA_WIDTH), lambda i: (i, 0))),
        compiler_params=_params(("parallel",)),
        name="mla_prologue",
    )(x2, g, wcq, wckv, wkr, gcq, gckv, wuq, wuk, wuv, *tabs)


_NT = (((1,), (1,)), ((), ()))
N_STREAMS = 2


def _chunk(off, tk):
    return pl.ds(off if isinstance(off, int) else pl.multiple_of(off, tk), tk)


def _softmax_phase(st, slot):
    sbuf, pbuf, abuf, m_ref, l_ref, _ = st
    for z in range(N_STREAMS):
        s = sbuf[z, slot]
        tk = s.shape[1]
        m_prev = m_ref[z]
        m_new = jnp.maximum(m_prev, jnp.max(s, axis=1, keepdims=True))
        alpha = jnp.exp2(m_prev - m_new)
        p = jnp.exp2(s - jnp.concatenate([m_new] * (tk // LANES), axis=1))
        psum = p[:, :LANES]
        for c in range(1, tk // LANES):
            psum = psum + p[:, c * LANES:(c + 1) * LANES]
        m_ref[z] = m_new
        l_ref[z] = alpha * l_ref[z] + psum
        abuf[z, slot] = alpha
        pbuf[z, slot] = p.astype(BF16)


def _pv_phase(st, v, slot):
    _, pbuf, abuf, _, _, a_ref = st
    for z in range(N_STREAMS):
        a_ref[z] = abuf[z, slot] * a_ref[z] + jnp.dot(pbuf[z, slot], v, preferred_element_type=F32)


def _attend(qk_phase, v_of, nk, tk, st):
    assert nk % 2 == 0 and nk >= 2
    sbuf, pbuf, abuf, m_ref, l_ref, a_ref = st
    m_ref[...] = jnp.full(m_ref.shape, NEG_BIG, F32)
    l_ref[...] = jnp.zeros(l_ref.shape, F32)
    a_ref[...] = jnp.zeros(a_ref.shape, F32)
    qk_phase(0, 0)
    qk_phase(tk, 1)
    _softmax_phase(st, 0)

    def trip(j, carry):
        off = pl.multiple_of(2 * j * tk, 2 * tk)
        qk_phase(off + 2 * tk, 0)
        _pv_phase(st, v_of(off), 0)
        _softmax_phase(st, 1)
        qk_phase(off + 3 * tk, 1)
        _pv_phase(st, v_of(off + tk), 1)
        _softmax_phase(st, 0)
        return carry

    lax.fori_loop(0, (nk - 2) // 2, trip, 0)
    _pv_phase(st, v_of((nk - 2) * tk), 0)
    _softmax_phase(st, 1)
    _pv_phase(st, v_of((nk - 1) * tk), 1)


def _normalised(st, z):
    return st[5][z] / jnp.sum(st[4][z], axis=1, keepdims=True)


def _attn_scratch(tq, tk):
    return [pltpu.VMEM((N_STREAMS, 2, tq, tk), F32), pltpu.VMEM((N_STREAMS, 2, tq, tk), BF16),
            pltpu.VMEM((N_STREAMS, 2, tq, LANES), F32)] + [pltpu.VMEM((N_STREAMS, tq, LANES), F32)] * 3


def _diff_attn_kernel(lam_ref, gd_ref, q_ref, k_ref, v_ref, o_ref, *st, tk, lam_init):
    q = q_ref[0]
    tq = q.shape[0]
    lane = lax.broadcasted_iota(jnp.int32, q.shape, 1)
    zero = jnp.zeros_like(q)
    q12 = jnp.concatenate([jnp.where(lane < DIFF_HEAD_DIM, q, zero),
                           jnp.where(lane >= DIFF_HEAD_DIM, q, zero)], axis=0)

    def qk_phase(off, slot):
        s = lax.dot_general(q12, k_ref[0, _chunk(off, tk), :], _NT, preferred_element_type=F32)
        st[0][0, slot] = s[:tq]
        st[0][1, slot] = s[tq:]

    _attend(qk_phase, lambda off: v_ref[0, _chunk(off, tk), :], k_ref.shape[1] // tk, tk, st)

    t = lam_ref[...]
    lam = (jnp.exp(jnp.sum(t[0:1] * t[1:2], axis=1, keepdims=True))
           - jnp.exp(jnp.sum(t[2:3] * t[3:4], axis=1, keepdims=True)) + lam_init)
    o = _normalised(st, 0) - lam * _normalised(st, 1)
    o_ref[0] = (_rmsnorm(o, gd_ref[...]) * (1.0 - lam_init)).astype(BF16)


def _diff_attention(qkv, lam_p, g_diff, layer_idx, tq, tk):
    b, s, _ = qkv.shape
    kcol, vcol = DIFF_WIDTH // LANES, 2 * DIFF_WIDTH // LANES
    lam_init = 0.8 - 0.6 * math.exp(-0.3 * layer_idx)
    return pl.pallas_call(
        functools.partial(_diff_attn_kernel, tk=tk, lam_init=lam_init),
        out_shape=jax.ShapeDtypeStruct((b, s, DIFF_WIDTH), BF16),
        grid=(b, DIFF_HEADS, s // tq),
        in_specs=[pl.BlockSpec((4, DIFF_HEAD_DIM), lambda bi, h, qi: (0, 0)),
                  pl.BlockSpec((1, DIFF_V_DIM), lambda bi, h, qi: (0, 0)),
                  pl.BlockSpec((1, tq, LANES), lambda bi, h, qi: (bi, qi, h)),
                  pl.BlockSpec((1, s, LANES), lambda bi, h, qi: (bi, 0, kcol + h)),
                  pl.BlockSpec((1, s, LANES), lambda bi, h, qi: (bi, 0, vcol + h))],
        out_specs=pl.BlockSpec((1, tq, LANES), lambda bi, h, qi: (bi, qi, h)),
        scratch_shapes=_attn_scratch(tq, tk),
        compiler_params=_params(("parallel", "parallel", "arbitrary")),
        name="diff_attention",
    )(lam_p, g_diff, qkv, qkv, qkv)


def _mla_attn_kernel(q_ref, k_ref, v_ref, o_ref, *st, tk):
    qs = (q_ref[0, :, :MLA_PAD_DIM], q_ref[0, :, MLA_PAD_DIM:])

    def qk_phase(off, slot):
        for z in range(N_STREAMS):
            kz = k_ref[0, _chunk(off, tk), z * MLA_PAD_DIM:(z + 1) * MLA_PAD_DIM]
            st[0][z, slot] = lax.dot_general(qs[z], kz, _NT, preferred_element_type=F32)

    _attend(qk_phase, lambda off: v_ref[0, _chunk(off, tk), :], k_ref.shape[1] // tk, tk, st)
    lane = lax.broadcasted_iota(jnp.int32, o_ref.shape[1:], 1)
    o_ref[0] = jnp.where(lane < MLA_V_DIM, _normalised(st, 0), _normalised(st, 1)).astype(BF16)


def _mla_attention(q, k, v, tq, tk):
    b, s, _ = q.shape
    return pl.pallas_call(
        functools.partial(_mla_attn_kernel, tk=tk),
        out_shape=jax.ShapeDtypeStruct((b, s, MLA_WIDTH), BF16),
        grid=(b, MLA_HEADS // 2, s // tq),
        in_specs=[pl.BlockSpec((1, tq, 2 * MLA_PAD_DIM), lambda bi, h, qi: (bi, qi, h)),
                  pl.BlockSpec((1, s, 2 * MLA_PAD_DIM), lambda bi, h, qi: (bi, 0, h)),
                  pl.BlockSpec((1, s, LANES), lambda bi, h, qi: (bi, 0, h))],
        out_specs=pl.BlockSpec((1, tq, LANES), lambda bi, h, qi: (bi, qi, h)),
        scratch_shapes=_attn_scratch(tq, tk),
        compiler_params=_params(("parallel", "parallel", "arbitrary")),
        name="mla_attention",
    )(q, k, v)


def _dil_attn_kernel(q_ref, k_ref, v_ref, o_ref, lse_ref, *, tq, win):
    sub_len = k_ref.shape[1]
    lane = lax.broadcasted_iota(jnp.int32, (tq, DIL_WIDTH), 1)
    heads = [(lane // DIL_HEAD_DIM) == h for h in range(DIL_HEADS)]
    row = lax.broadcasted_iota(jnp.int32, (DIL_HEADS * tq, win), 0) % tq
    col = lax.broadcasted_iota(jnp.int32, (DIL_HEADS * tq, win), 1)
    for t in range(q_ref.shape[1] // tq):
        r0 = pl.program_id(2) * q_ref.shape[1] + t * tq
        start = pl.multiple_of(jnp.clip(r0 - DIL_SPAN, 0, sub_len - win), DIL_SPAN)
        q = q_ref[0, t * tq:(t + 1) * tq, :]
        kw = k_ref[0, pl.ds(start, win), :]
        vw = v_ref[0, pl.ds(start, win), :]
        band = jnp.abs(row - col + (r0 - start)) <= DIL_SPAN
        qz = jnp.concatenate([jnp.where(hm, q, jnp.zeros_like(q)) for hm in heads], axis=0)
        s = jnp.where(band, lax.dot_general(qz, kw, _NT, preferred_element_type=F32), NEG_BIG)
        m = jnp.max(s, axis=1, keepdims=True)
        p = jnp.exp2(s - m)
        l = jnp.sum(p, axis=1, keepdims=True)
        on = jnp.dot(p.astype(BF16), vw, preferred_element_type=F32) / l
        ls = m * LN2 + jnp.log(l)
        o = jnp.zeros(q.shape, F32)
        lse = jnp.zeros(q.shape, F32)
        for h, hm in enumerate(heads):
            o = jnp.where(hm, on[h * tq:(h + 1) * tq], o)
            lse = jnp.where(hm, ls[h * tq:(h + 1) * tq], lse)
        o_ref[0, t * tq:(t + 1) * tq, :] = o
        lse_ref[0, t * tq:(t + 1) * tq, :] = lse


def _dil_attention(qkv_g, dilation, group):
    b, s, _ = qkv_g.shape
    sub_len = s // dilation
    tq = min(128, sub_len)
    win = min(tq + 2 * DIL_SPAN, sub_len)
    rows = min(4 * tq, sub_len)
    sub = qkv_g.reshape(b, sub_len, dilation * 3 * DIL_WIDTH)
    out_sd = jax.ShapeDtypeStruct((b, sub_len, dilation * DIL_WIDTH), F32)
    o, lse = pl.pallas_call(
        functools.partial(_dil_attn_kernel, tq=tq, win=win),
        out_shape=(out_sd, out_sd),
        grid=(b, dilation, sub_len // rows),
        in_specs=[pl.BlockSpec((1, rows, DIL_WIDTH), lambda bi, c, qi: (bi, qi, 3 * c)),
                  pl.BlockSpec((1, sub_len, DIL_WIDTH), lambda bi, c, qi: (bi, 0, 3 * c + 1)),
                  pl.BlockSpec((1, sub_len, DIL_WIDTH), lambda bi, c, qi: (bi, 0, 3 * c + 2))],
        out_specs=(pl.BlockSpec((1, rows, DIL_WIDTH), lambda bi, c, qi: (bi, qi, c)),
                   pl.BlockSpec((1, rows, DIL_WIDTH), lambda bi, c, qi: (bi, qi, c))),
        compiler_params=_params(("parallel", "parallel", "arbitrary")),
        name="dil_attention_%d" % group,
    )(sub, sub, sub)
    return o.reshape(b * s, DIL_WIDTH), lse.reshape(b * s, DIL_WIDTH)


def _merge_kernel(x_ref, oa_ref, oc_ref, o0_ref, o1_ref, o2_ref, e0_ref, e1_ref, e2_ref, gt_ref,
                  wa_ref, wb_ref, wc_ref, wo_ref, y_ref):
    e0, e1, e2 = e0_ref[...], e1_ref[...], e2_ref[...]
    top = jnp.maximum(jnp.maximum(e0, e1), e2)
    p0, p1, p2 = jnp.exp(e0 - top), jnp.exp(e1 - top), jnp.exp(e2 - top)
    ob = (p0 * o0_ref[...] + p1 * o1_ref[...] + p2 * o2_ref[...]) / (p0 + p1 + p2)
    ya = jnp.dot(oa_ref[...], wa_ref[...], preferred_element_type=F32)
    yb = jnp.dot(ob.astype(BF16), wb_ref[...], preferred_element_type=F32)
    yc = jnp.dot(oc_ref[...], wc_ref[...], preferred_element_type=F32)
    merged = (gt_ref[:, :D_MODEL].astype(F32) * ya + gt_ref[:, D_MODEL:2 * D_MODEL].astype(F32) * yb
              + gt_ref[:, 2 * D_MODEL:].astype(F32) * yc)
    y_ref[...] = x_ref[...] + jnp.dot(merged.astype(BF16), wo_ref[...], preferred_element_type=F32)


def _merge_out(x2, oa, oc, dil, gates, wa, wb, wc, wo, tm):
    t = x2.shape[0]
    row = lambda w: pl.BlockSpec((tm, w), lambda i: (i, 0))
    full = lambda a: pl.BlockSpec(a.shape, lambda i: (0, 0))
    (o0, e0), (o1, e1), (o2, e2) = dil
    return pl.pallas_call(
        _merge_kernel,
        out_shape=jax.ShapeDtypeStruct((t, D_MODEL), F32),
        grid=(t // tm,),
        in_specs=[row(D_MODEL), row(DIFF_WIDTH), row(MLA_WIDTH)] + [row(DIL_WIDTH)] * 6 + [row(3 * D_MODEL)]
                 + [full(wa), full(wb), full(wc), full(wo)],
        out_specs=row(D_MODEL),
        compiler_params=_params(("parallel",)),
        name="merge_out",
    )(x2, oa, oc, o0, o1, o2, e0, e1, e2, gates, wa, wb, wc, wo)


def _mlp_kernel(x_ref, g_ref, wu_ref, wd_ref, gf_ref, y_ref, h_ref, acc_ref, *, final_norm):
    j = pl.program_id(1)

    @pl.when(j == 0)
    def _():
        h_ref[...] = _rmsnorm(x_ref[...], g_ref[...]).astype(BF16)
        acc_ref[...] = x_ref[...]

    u = jnp.dot(h_ref[...], wu_ref[...], preferred_element_type=F32)
    u = jnp.square(jnp.maximum(u, 0.0)).astype(BF16)
    acc_ref[...] += jnp.dot(u, wd_ref[...], preferred_element_type=F32)

    @pl.when(j == pl.num_programs(1) - 1)
    def _():
        if final_norm:
            y_ref[...] = _rmsnorm(acc_ref[...], gf_ref[...])
        else:
            y_ref[...] = acc_ref[...]


def _mlp(x2, g, wu, wd, g_final, final_norm, tm, th):
    t = x2.shape[0]
    return pl.pallas_call(
        functools.partial(_mlp_kernel, final_norm=final_norm),
        out_shape=jax.ShapeDtypeStruct((t, D_MODEL), F32),
        grid=(t // tm, MLP_HIDDEN // th),
        in_specs=[pl.BlockSpec((tm, D_MODEL), lambda i, j: (i, 0)),
                  pl.BlockSpec((1, D_MODEL), lambda i, j: (0, 0)),
                  pl.BlockSpec((D_MODEL, th), lambda i, j: (0, j)),
                  pl.BlockSpec((th, D_MODEL), lambda i, j: (j, 0)),
                  pl.BlockSpec((1, D_MODEL), lambda i, j: (0, 0))],
        out_specs=pl.BlockSpec((tm, D_MODEL), lambda i, j: (i, 0)),
        scratch_shapes=[pltpu.VMEM((tm, D_MODEL), BF16), pltpu.VMEM((tm, D_MODEL), F32)],
        compiler_params=_params(("parallel", "arbitrary")),
        name="mlp",
    )(x2, g, wu, wd, g_final)


def _rope_tables(seq, dim, first_lane, period):
    half = dim // 2
    inv = ROPE_THETA ** (-jnp.arange(half, dtype=F32) / half)
    ang = jnp.arange(seq, dtype=F32)[:, None] * inv[None, :]
    lane = jnp.arange(LANES)
    rel = (lane % period) - first_lane
    active = (rel >= 0) & (rel < dim)
    idx = jnp.clip(rel, 0, dim - 1) % half
    cos = jnp.where(active[None, :], jnp.cos(ang)[:, idx], 1.0)
    sin = jnp.where(active[None, :], jnp.sin(ang)[:, idx], 0.0)
    first = (rel < half)[None, :]
    return cos, jnp.where(first, -sin, 0.0), jnp.where(first, 0.0, sin)


def _layer_weights(w_in, w_uq, w_ukv):
    def cols(a, n):
        return w_in[:, a:a + n]
    dq, dk, dv = cols(0, 512), cols(512, 512), cols(1024, 512)
    dil = [[cols(1536 + 768 * g + 256 * t, 256) for t in range(3)] for g in range(3)]
    w_qkv = jnp.concatenate([dq] + [dil[g][0] for g in range(3)] + [dk] + [dil[g][1] for g in range(3)]
                            + [dv] + [dil[g][2] for g in range(3)], axis=1).astype(BF16)
    base = 3840
    w_cq = cols(base, MLA_Q_LORA).astype(BF16)
    w_ckv = cols(base + MLA_Q_LORA, MLA_KV_LORA).astype(BF16)
    w_kr = cols(base + MLA_Q_LORA + MLA_KV_LORA, MLA_ROPE_DIM)
    w_kr = jnp.pad(w_kr, ((0, 0), (MLA_NOPE_DIM, MLA_PAD_DIM - MLA_NOPE_DIM - MLA_ROPE_DIM))).astype(BF16)
    w_gate = cols(base + MLA_Q_LORA + MLA_KV_LORA + MLA_ROPE_DIM, 3 * D_MODEL).astype(BF16)
    qd = MLA_NOPE_DIM + MLA_ROPE_DIM
    w_uq_p = jnp.pad(w_uq.reshape(MLA_Q_LORA, MLA_HEADS, qd), ((0, 0), (0, 0), (0, MLA_PAD_DIM - qd)))
    w_uq_p = w_uq_p.reshape(MLA_Q_LORA, MLA_HEADS * MLA_PAD_DIM).astype(BF16)
    kv = w_ukv.reshape(MLA_KV_LORA, MLA_HEADS, MLA_NOPE_DIM + MLA_V_DIM)
    w_uk_p = jnp.pad(kv[:, :, :MLA_NOPE_DIM], ((0, 0), (0, 0), (0, MLA_PAD_DIM - MLA_NOPE_DIM)))
    w_uk_p = w_uk_p.reshape(MLA_KV_LORA, MLA_HEADS * MLA_PAD_DIM).astype(BF16)
    w_uv = kv[:, :, MLA_NOPE_DIM:].reshape(MLA_KV_LORA, MLA_WIDTH).astype(BF16)
    return w_qkv, w_cq, w_ckv, w_kr, w_gate, w_uq_p, w_uk_p, w_uv


def kernel(x, w_in, b_gate, g_mix, diff_lambda, g_diff, g_cq, g_ckv, w_uq, w_ukv, w_o_diff, w_o_dil, w_o_mla,
           w_out, g_mlp, w_up, w_down, g_final):
    b, s, d = x.shape
    assert d == D_MODEL and s % 1024 == 0
    depth = w_in.shape[0]
    tm = min(1024, s)
    tq, tk = 256, min(512, s)
    tabs64 = _rope_tables(s, DIFF_HEAD_DIM, 0, DIFF_HEAD_DIM)
    tabs_mla = _rope_tables(s, MLA_ROPE_DIM, MLA_NOPE_DIM, MLA_PAD_DIM)
    x2 = x.reshape(b * s, d)
    for l in range(depth):
        w_qkv, w_cq, w_ckv, w_kr, w_gate, w_uq_p, w_uk_p, w_uv = _layer_weights(w_in[l], w_uq[l], w_ukv[l])
        g = g_mix[l][None, :]
        qkv_diff, *qkv_dil = _qkv_proj(x2, g, w_qkv, tabs64, s, tm)
        gates = _gate_proj(x2, g, w_gate, b_gate[l][:, None, :], tm)
        q_c, k_c, v_c = _mla_prologue(x2, g, w_cq, w_ckv, w_kr, g_cq[l][None, :], g_ckv[l][None, :],
                                      w_uq_p, w_uk_p, w_uv, tabs_mla, s, min(512, tm))
        o_a = _diff_attention(qkv_diff.reshape(b, s, 3 * DIFF_WIDTH), diff_lambda[l], g_diff[l][None, :], l, tq, tk)
        dil = [_dil_attention(qkv_dil[gi].reshape(b, s, 3 * DIL_WIDTH), dilation, gi)
               for gi, (_, dilation) in enumerate(DIL_PATTERNS)]
        hp = MLA_HEADS * MLA_PAD_DIM
        o_c = _mla_attention(q_c.reshape(b, s, hp), k_c.reshape(b, s, hp), v_c.reshape(b, s, MLA_WIDTH), tq, tk)
        x2 = _merge_out(x2, o_a.reshape(b * s, DIFF_WIDTH), o_c.reshape(b * s, MLA_WIDTH), dil, gates,
                        w_o_diff[l].astype(BF16), w_o_dil[l].astype(BF16), w_o_mla[l].astype(BF16),
                        w_out[l].astype(BF16), min(512, tm))
        x2 = _mlp(x2, g_mlp[l][None, :], w_up[l].astype(BF16), w_down[l].astype(BF16), g_final[None, :],
                  l == depth - 1, tm, 1024)
    return x2.reshape(b, s, d)
```

```python
import functools
import math

import jax
import jax.numpy as jnp
from jax import lax
from jax.experimental import pallas as pl
from jax.experimental.pallas import tpu as pltpu

F32 = jnp.float32
BF16 = jnp.bfloat16

D_MODEL = 1024
ROPE_THETA = 10000.0
NORM_EPS = 1e-6
NEG_BIG = -1e30
LOG2E = math.log2(math.e)
LN2 = math.log(2.0)

DIFF_HEADS = 4
DIFF_HEAD_DIM = 64
DIFF_V_DIM = 128
DIFF_WIDTH = 512

DIL_PATTERNS = ((128, 1), (512, 4), (2048, 16))
DIL_SPAN = 64
DIL_HEADS = 4
DIL_HEAD_DIM = 64
DIL_WIDTH = 256

MLA_HEADS = 8
MLA_NOPE_DIM = 64
MLA_ROPE_DIM = 32
MLA_V_DIM = 64
MLA_Q_LORA = 768
MLA_KV_LORA = 256
MLA_WIDTH = 512
MLA_PAD_DIM = 128

MLP_HIDDEN = 4096
LANES = 128

QKV_GROUP = 1280
VMEM_LIMIT = 56 * 1024 * 1024


def _params(semantics):
    return pltpu.CompilerParams(dimension_semantics=semantics, vmem_limit_bytes=VMEM_LIMIT)


def _rmsnorm(x, g):
    ms = jnp.mean(x * x, axis=-1, keepdims=True)
    return x * lax.rsqrt(ms + NORM_EPS) * g


def _rope_chunk(a, c, sa, sb, half):
    return a * c + pltpu.roll(a, LANES - half, 1) * sa + pltpu.roll(a, half, 1) * sb


def _qkv_kernel(x_ref, g_ref, w_ref, c_ref, sa_ref, sb_ref, od_ref, o0_ref, o1_ref, o2_ref, h_ref, *, qscale):
    j = pl.program_id(1)

    @pl.when(j == 0)
    def _():
        h_ref[...] = _rmsnorm(x_ref[...], g_ref[...]).astype(BF16)

    acc = jnp.dot(h_ref[...], w_ref[...], preferred_element_type=F32)

    def emit(fn):
        for ch in range(QKV_GROUP // LANES):
            val = fn(acc[:, ch * LANES:(ch + 1) * LANES]).astype(BF16)
            if ch < DIFF_WIDTH // LANES:
                od_ref[:, ch * LANES:(ch + 1) * LANES] = val
            else:
                r = ch - DIFF_WIDTH // LANES
                dst = (o0_ref, o1_ref, o2_ref)[r // 2]
                dst[:, (r % 2) * LANES:(r % 2 + 1) * LANES] = val

    @pl.when(j == 2)
    def _():
        emit(lambda a: a)

    @pl.when(j < 2)
    def _():
        scale = jnp.where(j == 0, qscale, 1.0).astype(F32)
        c = c_ref[...] * scale
        sa = sa_ref[...] * scale
        sb = sb_ref[...] * scale
        emit(lambda a: _rope_chunk(a, c, sa, sb, DIFF_HEAD_DIM // 2))


def _qkv_proj(x2, g, w, tabs, seq, tm):
    t = x2.shape[0]
    nrow = seq // tm
    tab_spec = pl.BlockSpec((tm, LANES), lambda i, j: (i % nrow, 0))
    dil_sd = jax.ShapeDtypeStruct((t, 3 * DIL_WIDTH), BF16)
    dil_spec = pl.BlockSpec((tm, DIL_WIDTH), lambda i, j: (i, j))
    return pl.pallas_call(
        functools.partial(_qkv_kernel, qscale=DIFF_HEAD_DIM ** -0.5 * LOG2E),
        out_shape=(jax.ShapeDtypeStruct((t, 3 * DIFF_WIDTH), BF16), dil_sd, dil_sd, dil_sd),
        grid=(t // tm, 3),
        in_specs=[pl.BlockSpec((tm, D_MODEL), lambda i, j: (i, 0)),
                  pl.BlockSpec((1, D_MODEL), lambda i, j: (0, 0)),
                  pl.BlockSpec((D_MODEL, QKV_GROUP), lambda i, j: (0, j)),
                  tab_spec, tab_spec, tab_spec],
        out_specs=(pl.BlockSpec((tm, DIFF_WIDTH), lambda i, j: (i, j)), dil_spec, dil_spec, dil_spec),
        scratch_shapes=[pltpu.VMEM((tm, D_MODEL), BF16)],
        compiler_params=_params(("parallel", "arbitrary")),
        name="qkv_proj",
    )(x2, g, w, *tabs)


def _gate_kernel(x_ref, g_ref, w_ref, b_ref, o_ref, h_ref):
    @pl.when(pl.program_id(1) == 0)
    def _():
        h_ref[...] = _rmsnorm(x_ref[...], g_ref[...]).astype(BF16)

    z = jnp.dot(h_ref[...], w_ref[...], preferred_element_type=F32) + b_ref[0]
    o_ref[...] = (1.0 / (1.0 + jnp.exp(-z))).astype(BF16)


def _gate_proj(x2, g, w, b, tm):
    t = x2.shape[0]
    return pl.pallas_call(
        _gate_kernel,
        out_shape=jax.ShapeDtypeStruct((t, 3 * D_MODEL), BF16),
        grid=(t // tm, 3),
        in_specs=[pl.BlockSpec((tm, D_MODEL), lambda i, j: (i, 0)),
                  pl.BlockSpec((1, D_MODEL), lambda i, j: (0, 0)),
                  pl.BlockSpec((D_MODEL, D_MODEL), lambda i, j: (0, j)),
                  pl.BlockSpec((1, 1, D_MODEL), lambda i, j: (j, 0, 0))],
        out_specs=pl.BlockSpec((tm, D_MODEL), lambda i, j: (i, j)),
        scratch_shapes=[pltpu.VMEM((tm, D_MODEL), BF16)],
        compiler_params=_params(("parallel", "arbitrary")),
        name="gate_proj",
    )(x2, g, w, b)


def _mla_pro_kernel(x_ref, g_ref, wcq_ref, wckv_ref, wkr_ref, gcq_ref, gckv_ref, wuq_ref, wuk_ref, wuv_ref,
                    c_ref, sa_ref, sb_ref, q_ref, k_ref, v_ref, *, qscale):
    h = _rmsnorm(x_ref[...], g_ref[...]).astype(BF16)
    cq = jnp.dot(h, wcq_ref[...], preferred_element_type=F32)
    cq = _rmsnorm(cq, gcq_ref[...]).astype(BF16)
    ckv = jnp.dot(h, wckv_ref[...], preferred_element_type=F32)
    ckv = _rmsnorm(ckv, gckv_ref[...]).astype(BF16)
    c, sa, sb = c_ref[...], sa_ref[...], sb_ref[...]
    half = MLA_ROPE_DIM // 2
    kr = _rope_chunk(jnp.dot(h, wkr_ref[...], preferred_element_type=F32), c, sa, sb, half)
    qh = jnp.dot(cq, wuq_ref[...], preferred_element_type=F32)
    kn = jnp.dot(ckv, wuk_ref[...], preferred_element_type=F32)
    cq_s, sa_s, sb_s = c * qscale, sa * qscale, sb * qscale
    for hd in range(MLA_HEADS):
        sl = slice(hd * MLA_PAD_DIM, (hd + 1) * MLA_PAD_DIM)
        q_ref[:, sl] = _rope_chunk(qh[:, sl], cq_s, sa_s, sb_s, half).astype(BF16)
        k_ref[:, sl] = (kn[:, sl] + kr).astype(BF16)
    v_ref[...] = jnp.dot(ckv, wuv_ref[...], preferred_element_type=F32).astype(BF16)


def _mla_prologue(x2, g, wcq, wckv, wkr, gcq, gckv, wuq, wuk, wuv, tabs, seq, tm):
    t = x2.shape[0]
    nrow = seq // tm
    full = lambda a: pl.BlockSpec(a.shape, lambda i: (0,) * a.ndim)
    tab_spec = pl.BlockSpec((tm, LANES), lambda i: (i % nrow, 0))
    hp = MLA_HEADS * MLA_PAD_DIM
    return pl.pallas_call(
        functools.partial(_mla_pro_kernel, qscale=(MLA_NOPE_DIM + MLA_ROPE_DIM) ** -0.5 * LOG2E),
        out_shape=(jax.ShapeDtypeStruct((t, hp), BF16), jax.ShapeDtypeStruct((t, hp), BF16),
                   jax.ShapeDtypeStruct((t, MLA_WIDTH), BF16)),
        grid=(t // tm,),
        in_specs=[pl.BlockSpec((tm, D_MODEL), lambda i: (i, 0)), full(g), full(wcq), full(wckv), full(wkr),
                  full(gcq), full(gckv), full(wuq), full(wuk), full(wuv), tab_spec, tab_spec, tab_spec],
        out_specs=(pl.BlockSpec((tm, hp), lambda i: (i, 0)), pl.BlockSpec((tm, hp), lambda i: (i, 0)),
                   pl.BlockSpec((tm, MLA_WIDTH), lambda i: (i, 0))),
        compiler_params=_params(("parallel",)),
        name="mla_prologue",
    )(x2, g, wcq, wckv, wkr, gcq, gckv, wuq, wuk, wuv, *tabs)


_NT = (((1,), (1,)), ((), ()))
N_STREAMS = 2


def _chunk(off, tk):
    return pl.ds(off if isinstance(off, int) else pl.multiple_of(off, tk), tk)


def _softmax_phase(st, slot):
    sbuf, pbuf, abuf, m_ref, l_ref, _ = st
    for z in range(N_STREAMS):
        s = sbuf[z, slot]
        tk = s.shape[1]
        m_prev = m_ref[z]
        m_new = jnp.maximum(m_prev, jnp.max(s, axis=1, keepdims=True))
        alpha = jnp.exp2(m_prev - m_new)
        p = jnp.exp2(s - jnp.concatenate([m_new] * (tk // LANES), axis=1))
        psum = p[:, :LANES]
        for c in range(1, tk // LANES):
            psum = psum + p[:, c * LANES:(c + 1) * LANES]
        m_ref[z] = m_new
        l_ref[z] = alpha * l_ref[z] + psum
        abuf[z, slot] = alpha
        pbuf[z, slot] = p.astype(BF16)


def _pv_phase(st, v, slot):
    _, pbuf, abuf, _, _, a_ref = st
    for z in range(N_STREAMS):
        a_ref[z] = abuf[z, slot] * a_ref[z] + jnp.dot(pbuf[z, slot], v, preferred_element_type=F32)


def _attend(qk_phase, v_of, nk, tk, st):
    assert nk % 2 == 0 and nk >= 2
    sbuf, pbuf, abuf, m_ref, l_ref, a_ref = st
    m_ref[...] = jnp.full(m_ref.shape, NEG_BIG, F32)
    l_ref[...] = jnp.zeros(l_ref.shape, F32)
    a_ref[...] = jnp.zeros(a_ref.shape, F32)
    qk_phase(0, 0)
    for i in range(nk):
        if i + 1 < nk:
            qk_phase((i + 1) * tk, (i + 1) % 2)
        if i > 0:
            _pv_phase(st, v_of((i - 1) * tk), (i - 1) % 2)
        _softmax_phase(st, i % 2)
    _pv_phase(st, v_of((nk - 1) * tk), (nk - 1) % 2)


def _normalised(st, z):
    return st[5][z] / jnp.sum(st[4][z], axis=1, keepdims=True)


def _attn_scratch(tq, tk):
    return [pltpu.VMEM((N_STREAMS, 2, tq, tk), F32), pltpu.VMEM((N_STREAMS, 2, tq, tk), BF16),
            pltpu.VMEM((N_STREAMS, 2, tq, LANES), F32)] + [pltpu.VMEM((N_STREAMS, tq, LANES), F32)] * 3


def _diff_attn_kernel(lam_ref, gd_ref, q_ref, k_ref, v_ref, o_ref, *st, tk, lam_init):
    q = q_ref[0]
    tq = q.shape[0]
    lane = lax.broadcasted_iota(jnp.int32, q.shape, 1)
    zero = jnp.zeros_like(q)
    q12 = jnp.concatenate([jnp.where(lane < DIFF_HEAD_DIM, q, zero),
                           jnp.where(lane >= DIFF_HEAD_DIM, q, zero)], axis=0)

    def qk_phase(off, slot):
        s = lax.dot_general(q12, k_ref[0, _chunk(off, tk), :], _NT, preferred_element_type=F32)
        st[0][0, slot] = s[:tq]
        st[0][1, slot] = s[tq:]

    _attend(qk_phase, lambda off: v_ref[0, _chunk(off, tk), :], k_ref.shape[1] // tk, tk, st)

    t = lam_ref[...]
    lam = (jnp.exp(jnp.sum(t[0:1] * t[1:2], axis=1, keepdims=True))
           - jnp.exp(jnp.sum(t[2:3] * t[3:4], axis=1, keepdims=True)) + lam_init)
    o = _normalised(st, 0) - lam * _normalised(st, 1)
    o_ref[0] = (_rmsnorm(o, gd_ref[...]) * (1.0 - lam_init)).astype(BF16)


def _diff_attention(qkv, lam_p, g_diff, layer_idx, tq, tk):
    b, s, _ = qkv.shape
    kcol, vcol = DIFF_WIDTH // LANES, 2 * DIFF_WIDTH // LANES
    lam_init = 0.8 - 0.6 * math.exp(-0.3 * layer_idx)
    return pl.pallas_call(
        functools.partial(_diff_attn_kernel, tk=tk, lam_init=lam_init),
        out_shape=jax.ShapeDtypeStruct((b, s, DIFF_WIDTH), BF16),
        grid=(b, DIFF_HEADS, s // tq),
        in_specs=[pl.BlockSpec((4, DIFF_HEAD_DIM), lambda bi, h, qi: (0, 0)),
                  pl.BlockSpec((1, DIFF_V_DIM), lambda bi, h, qi: (0, 0)),
                  pl.BlockSpec((1, tq, LANES), lambda bi, h, qi: (bi, qi, h)),
                  pl.BlockSpec((1, s, LANES), lambda bi, h, qi: (bi, 0, kcol + h)),
                  pl.BlockSpec((1, s, LANES), lambda bi, h, qi: (bi, 0, vcol + h))],
        out_specs=pl.BlockSpec((1, tq, LANES), lambda bi, h, qi: (bi, qi, h)),
        scratch_shapes=_attn_scratch(tq, tk),
        compiler_params=_params(("parallel", "parallel", "arbitrary")),
        name="diff_attention",
    )(lam_p, g_diff, qkv, qkv, qkv)


def _mla_attn_kernel(q_ref, k_ref, v_ref, o_ref, *st, tk):
    qs = (q_ref[0, :, :MLA_PAD_DIM], q_ref[0, :, MLA_PAD_DIM:])

    def qk_phase(off, slot):
        for z in range(N_STREAMS):
            kz = k_ref[0, _chunk(off, tk), z * MLA_PAD_DIM:(z + 1) * MLA_PAD_DIM]
            st[0][z, slot] = lax.dot_general(qs[z], kz, _NT, preferred_element_type=F32)

    _attend(qk_phase, lambda off: v_ref[0, _chunk(off, tk), :], k_ref.shape[1] // tk, tk, st)
    lane = lax.broadcasted_iota(jnp.int32, o_ref.shape[1:], 1)
    o_ref[0] = jnp.where(lane < MLA_V_DIM, _normalised(st, 0), _normalised(st, 1)).astype(BF16)


def _mla_attention(q, k, v, tq, tk):
    b, s, _ = q.shape
    return pl.pallas_call(
        functools.partial(_mla_attn_kernel, tk=tk),
        out_shape=jax.ShapeDtypeStruct((b, s, MLA_WIDTH), BF16),
        grid=(b, MLA_HEADS // 2, s // tq),
        in_specs=[pl.BlockSpec((1, tq, 2 * MLA_PAD_DIM), lambda bi, h, qi: (bi, qi, h)),
                  pl.BlockSpec((1, s, 2 * MLA_PAD_DIM), lambda bi, h, qi: (bi, 0, h)),
                  pl.BlockSpec((1, s, LANES), lambda bi, h, qi: (bi, 0, h))],
        out_specs=pl.BlockSpec((1, tq, LANES), lambda bi, h, qi: (bi, qi, h)),
        scratch_shapes=_attn_scratch(tq, tk),
        compiler_params=_params(("parallel", "parallel", "arbitrary")),
        name="mla_attention",
    )(q, k, v)


def _dil_attn_kernel(q_ref, k_ref, v_ref, o_ref, lse_ref, *, tq, win):
    sub_len = k_ref.shape[1]
    lane = lax.broadcasted_iota(jnp.int32, (tq, DIL_WIDTH), 1)
    heads = [(lane // DIL_HEAD_DIM) == h for h in range(DIL_HEADS)]
    row = lax.broadcasted_iota(jnp.int32, (DIL_HEADS * tq, win), 0) % tq
    col = lax.broadcasted_iota(jnp.int32, (DIL_HEADS * tq, win), 1)
    for t in range(q_ref.shape[1] // tq):
        r0 = pl.program_id(2) * q_ref.shape[1] + t * tq
        start = pl.multiple_of(jnp.clip(r0 - DIL_SPAN, 0, sub_len - win), DIL_SPAN)
        q = q_ref[0, t * tq:(t + 1) * tq, :]
        kw = k_ref[0, pl.ds(start, win), :]
        vw = v_ref[0, pl.ds(start, win), :]
        band = jnp.abs(row - col + (r0 - start)) <= DIL_SPAN
        qz = jnp.concatenate([jnp.where(hm, q, jnp.zeros_like(q)) for hm in heads], axis=0)
        s = jnp.where(band, lax.dot_general(qz, kw, _NT, preferred_element_type=F32), NEG_BIG)
        m = jnp.max(s, axis=1, keepdims=True)
        p = jnp.exp2(s - m)
        l = jnp.sum(p, axis=1, keepdims=True)
        on = jnp.dot(p.astype(BF16), vw, preferred_element_type=F32) / l
        ls = m * LN2 + jnp.log(l)
        o = jnp.zeros(q.shape, F32)
        lse = jnp.zeros(q.shape, F32)
        for h, hm in enumerate(heads):
            o = jnp.where(hm, on[h * tq:(h + 1) * tq], o)
            lse = jnp.where(hm, ls[h * tq:(h + 1) * tq], lse)
        o_ref[0, t * tq:(t + 1) * tq, :] = o
        lse_ref[0, t * tq:(t + 1) * tq, :] = lse


def _dil_attention(qkv_g, dilation, group):
    b, s, _ = qkv_g.shape
    sub_len = s // dilation
    tq = min(128, sub_len)
    win = min(tq + 2 * DIL_SPAN, sub_len)
    rows = min(4 * tq, sub_len)
    sub = qkv_g.reshape(b, sub_len, dilation * 3 * DIL_WIDTH)
    out_sd = jax.ShapeDtypeStruct((b, sub_len, dilation * DIL_WIDTH), F32)
    o, lse = pl.pallas_call(
        functools.partial(_dil_attn_kernel, tq=tq, win=win),
        out_shape=(out_sd, out_sd),
        grid=(b, dilation, sub_len // rows),
        in_specs=[pl.BlockSpec((1, rows, DIL_WIDTH), lambda bi, c, qi: (bi, qi, 3 * c)),
                  pl.BlockSpec((1, sub_len, DIL_WIDTH), lambda bi, c, qi: (bi, 0, 3 * c + 1)),
                  pl.BlockSpec((1, sub_len, DIL_WIDTH), lambda bi, c, qi: (bi, 0, 3 * c + 2))],
        out_specs=(pl.BlockSpec((1, rows, DIL_WIDTH), lambda bi, c, qi: (bi, qi, c)),
                   pl.BlockSpec((1, rows, DIL_WIDTH), lambda bi, c, qi: (bi, qi, c))),
        compiler_params=_params(("parallel", "parallel", "arbitrary")),
        name="dil_attention_%d" % group,
    )(sub, sub, sub)
    return o.reshape(b * s, DIL_WIDTH), lse.reshape(b * s, DIL_WIDTH)


def _merge_kernel(x_ref, oa_ref, oc_ref, o0_ref, o1_ref, o2_ref, e0_ref, e1_ref, e2_ref, gt_ref,
                  wa_ref, wb_ref, wc_ref, wo_ref, y_ref):
    e0, e1, e2 = e0_ref[...], e1_ref[...], e2_ref[...]
    top = jnp.maximum(jnp.maximum(e0, e1), e2)
    p0, p1, p2 = jnp.exp(e0 - top), jnp.exp(e1 - top), jnp.exp(e2 - top)
    ob = (p0 * o0_ref[...] + p1 * o1_ref[...] + p2 * o2_ref[...]) / (p0 + p1 + p2)
    ya = jnp.dot(oa_ref[...], wa_ref[...], preferred_element_type=F32)
    yb = jnp.dot(ob.astype(BF16), wb_ref[...], preferred_element_type=F32)
    yc = jnp.dot(oc_ref[...], wc_ref[...], preferred_element_type=F32)
    merged = (gt_ref[:, :D_MODEL].astype(F32) * ya + gt_ref[:, D_MODEL:2 * D_MODEL].astype(F32) * yb
              + gt_ref[:, 2 * D_MODEL:].astype(F32) * yc)
    y_ref[...] = x_ref[...] + jnp.dot(merged.astype(BF16), wo_ref[...], preferred_element_type=F32)


def _merge_out(x2, oa, oc, dil, gates, wa, wb, wc, wo, tm):
    t = x2.shape[0]
    row = lambda w: pl.BlockSpec((tm, w), lambda i: (i, 0))
    full = lambda a: pl.BlockSpec(a.shape, lambda i: (0, 0))
    (o0, e0), (o1, e1), (o2, e2) = dil
    return pl.pallas_call(
        _merge_kernel,
        out_shape=jax.ShapeDtypeStruct((t, D_MODEL), F32),
        grid=(t // tm,),
        in_specs=[row(D_MODEL), row(DIFF_WIDTH), row(MLA_WIDTH)] + [row(DIL_WIDTH)] * 6 + [row(3 * D_MODEL)]
                 + [full(wa), full(wb), full(wc), full(wo)],
        out_specs=row(D_MODEL),
        compiler_params=_params(("parallel",)),
        name="merge_out",
    )(x2, oa, oc, o0, o1, o2, e0, e1, e2, gates, wa, wb, wc, wo)


def _mlp_kernel(x_ref, g_ref, wu_ref, wd_ref, gf_ref, y_ref, h_ref, acc_ref, *, final_norm):
    j = pl.program_id(1)

    @pl.when(j == 0)
    def _():
        h_ref[...] = _rmsnorm(x_ref[...], g_ref[...]).astype(BF16)
        acc_ref[...] = x_ref[...]

    u = jnp.dot(h_ref[...], wu_ref[...], preferred_element_type=F32)
    u = jnp.square(jnp.maximum(u, 0.0)).astype(BF16)
    acc_ref[...] += jnp.dot(u, wd_ref[...], preferred_element_type=F32)

    @pl.when(j == pl.num_programs(1) - 1)
    def _():
        if final_norm:
            y_ref[...] = _rmsnorm(acc_ref[...], gf_ref[...])
        else:
            y_ref[...] = acc_ref[...]


def _mlp(x2, g, wu, wd, g_final, final_norm, tm, th):
    t = x2.shape[0]
    return pl.pallas_call(
        functools.partial(_mlp_kernel, final_norm=final_norm),
        out_shape=jax.ShapeDtypeStruct((t, D_MODEL), F32),
        grid=(t // tm, MLP_HIDDEN // th),
        in_specs=[pl.BlockSpec((tm, D_MODEL), lambda i, j: (i, 0)),
                  pl.BlockSpec((1, D_MODEL), lambda i, j: (0, 0)),
                  pl.BlockSpec((D_MODEL, th), lambda i, j: (0, j)),
                  pl.BlockSpec((th, D_MODEL), lambda i, j: (j, 0)),
                  pl.BlockSpec((1, D_MODEL), lambda i, j: (0, 0))],
        out_specs=pl.BlockSpec((tm, D_MODEL), lambda i, j: (i, 0)),
        scratch_shapes=[pltpu.VMEM((tm, D_MODEL), BF16), pltpu.VMEM((tm, D_MODEL), F32)],
        compiler_params=_params(("parallel", "arbitrary")),
        name="mlp",
    )(x2, g, wu, wd, g_final)


def _rope_tables(seq, dim, first_lane, period):
    half = dim // 2
    inv = ROPE_THETA ** (-jnp.arange(half, dtype=F32) / half)
    ang = jnp.arange(seq, dtype=F32)[:, None] * inv[None, :]
    lane = jnp.arange(LANES)
    rel = (lane % period) - first_lane
    active = (rel >= 0) & (rel < dim)
    idx = jnp.clip(rel, 0, dim - 1) % half
    cos = jnp.where(active[None, :], jnp.cos(ang)[:, idx], 1.0)
    sin = jnp.where(active[None, :], jnp.sin(ang)[:, idx], 0.0)
    first = (rel < half)[None, :]
    return cos, jnp.where(first, -sin, 0.0), jnp.where(first, 0.0, sin)


def _layer_weights(w_in, w_uq, w_ukv):
    def cols(a, n):
        return w_in[:, a:a + n]
    dq, dk, dv = cols(0, 512), cols(512, 512), cols(1024, 512)
    dil = [[cols(1536 + 768 * g + 256 * t, 256) for t in range(3)] for g in range(3)]
    w_qkv = jnp.concatenate([dq] + [dil[g][0] for g in range(3)] + [dk] + [dil[g][1] for g in range(3)]
                            + [dv] + [dil[g][2] for g in range(3)], axis=1).astype(BF16)
    base = 3840
    w_cq = cols(base, MLA_Q_LORA).astype(BF16)
    w_ckv = cols(base + MLA_Q_LORA, MLA_KV_LORA).astype(BF16)
    w_kr = cols(base + MLA_Q_LORA + MLA_KV_LORA, MLA_ROPE_DIM)
    w_kr = jnp.pad(w_kr, ((0, 0), (MLA_NOPE_DIM, MLA_PAD_DIM - MLA_NOPE_DIM - MLA_ROPE_DIM))).astype(BF16)
    w_gate = cols(base + MLA_Q_LORA + MLA_KV_LORA + MLA_ROPE_DIM, 3 * D_MODEL).astype(BF16)
    qd = MLA_NOPE_DIM + MLA_ROPE_DIM
    w_uq_p = jnp.pad(w_uq.reshape(MLA_Q_LORA, MLA_HEADS, qd), ((0, 0), (0, 0), (0, MLA_PAD_DIM - qd)))
    w_uq_p = w_uq_p.reshape(MLA_Q_LORA, MLA_HEADS * MLA_PAD_DIM).astype(BF16)
    kv = w_ukv.reshape(MLA_KV_LORA, MLA_HEADS, MLA_NOPE_DIM + MLA_V_DIM)
    w_uk_p = jnp.pad(kv[:, :, :MLA_NOPE_DIM], ((0, 0), (0, 0), (0, MLA_PAD_DIM - MLA_NOPE_DIM)))
    w_uk_p = w_uk_p.reshape(MLA_KV_LORA, MLA_HEADS * MLA_PAD_DIM).astype(BF16)
    w_uv = kv[:, :, MLA_NOPE_DIM:].reshape(MLA_KV_LORA, MLA_WIDTH).astype(BF16)
    return w_qkv, w_cq, w_ckv, w_kr, w_gate, w_uq_p, w_uk_p, w_uv


def kernel(x, w_in, b_gate, g_mix, diff_lambda, g_diff, g_cq, g_ckv, w_uq, w_ukv, w_o_diff, w_o_dil, w_o_mla,
           w_out, g_mlp, w_up, w_down, g_final):
    b, s, d = x.shape
    assert d == D_MODEL and s % 1024 == 0
    depth = w_in.shape[0]
    tm = min(1024, s)
    tq, tk = 256, min(512, s)
    tabs64 = _rope_tables(s, DIFF_HEAD_DIM, 0, DIFF_HEAD_DIM)
    tabs_mla = _rope_tables(s, MLA_ROPE_DIM, MLA_NOPE_DIM, MLA_PAD_DIM)
    x2 = x.reshape(b * s, d)
    for l in range(depth):
        w_qkv, w_cq, w_ckv, w_kr, w_gate, w_uq_p, w_uk_p, w_uv = _layer_weights(w_in[l], w_uq[l], w_ukv[l])
        g = g_mix[l][None, :]
        qkv_diff, *qkv_dil = _qkv_proj(x2, g, w_qkv, tabs64, s, tm)
        gates = _gate_proj(x2, g, w_gate, b_gate[l][:, None, :], tm)
        q_c, k_c, v_c = _mla_prologue(x2, g, w_cq, w_ckv, w_kr, g_cq[l][None, :], g_ckv[l][None, :],
                                      w_uq_p, w_uk_p, w_uv, tabs_mla, s, min(512, tm))
        o_a = _diff_attention(qkv_diff.reshape(b, s, 3 * DIFF_WIDTH), diff_lambda[l], g_diff[l][None, :], l, tq, tk)
        dil = [_dil_attention(qkv_dil[gi].reshape(b, s, 3 * DIL_WIDTH), dilation, gi)
               for gi, (_, dilation) in enumerate(DIL_PATTERNS)]
        hp = MLA_HEADS * MLA_PAD_DIM
        o_c = _mla_attention(q_c.reshape(b, s, hp), k_c.reshape(b, s, hp), v_c.reshape(b, s, MLA_WIDTH), tq, tk)
        x2 = _merge_out(x2, o_a.reshape(b * s, DIFF_WIDTH), o_c.reshape(b * s, MLA_WIDTH), dil, gates,
                        w_o_diff[l].astype(BF16), w_o_dil[l].astype(BF16), w_o_mla[l].astype(BF16),
                        w_out[l].astype(BF16), min(512, tm))
        x2 = _mlp(x2, g_mlp[l][None, :], w_up[l].astype(BF16), w_down[l].astype(BF16), g_final[None, :],
                  l == depth - 1, tm, 1024)
    return x2.reshape(b, s, d)
```

```python
import functools
import math

import jax
import jax.numpy as jnp
from jax import lax
from jax.experimental import pallas as pl
from jax.experimental.pallas import tpu as pltpu

F32 = jnp.float32
BF16 = jnp.bfloat16

D_MODEL = 1024
ROPE_THETA = 10000.0
NORM_EPS = 1e-6
NEG_BIG = -1e30
LOG2E = math.log2(math.e)
LN2 = math.log(2.0)

DIFF_HEADS = 4
DIFF_HEAD_DIM = 64
DIFF_V_DIM = 128
DIFF_WIDTH = 512

DIL_PATTERNS = ((128, 1), (512, 4), (2048, 16))
DIL_SPAN = 64
DIL_HEADS = 4
DIL_HEAD_DIM = 64
DIL_WIDTH = 256

MLA_HEADS = 8
MLA_NOPE_DIM = 64
MLA_ROPE_DIM = 32
MLA_V_DIM = 64
MLA_Q_LORA = 768
MLA_KV_LORA = 256
MLA_WIDTH = 512
MLA_PAD_DIM = 128

MLP_HIDDEN = 4096
LANES = 128
MXU_COLS = 256

QKV_GROUP = 1280
VMEM_LIMIT = 56 * 1024 * 1024


def _params(semantics):
    return pltpu.CompilerParams(dimension_semantics=semantics, vmem_limit_bytes=VMEM_LIMIT)


def _rmsnorm(x, g):
    ms = jnp.mean(x * x, axis=-1, keepdims=True)
    return x * lax.rsqrt(ms + NORM_EPS) * g


def _rope_chunk(a, c, sa, sb, half):
    return a * c + pltpu.roll(a, LANES - half, 1) * sa + pltpu.roll(a, half, 1) * sb


def _qkv_kernel(x_ref, g_ref, w_ref, c_ref, sa_ref, sb_ref, od_ref, o0_ref, o1_ref, o2_ref, h_ref, *, qscale):
    j = pl.program_id(1)

    @pl.when(j == 0)
    def _():
        h_ref[...] = _rmsnorm(x_ref[...], g_ref[...]).astype(BF16)

    scale = jnp.where(j == 0, qscale, 1.0).astype(F32)
    rot = jnp.where(j == 2, 0.0, scale).astype(F32)
    c = jnp.where(j == 2, 1.0, c_ref[...] * scale)
    sa = sa_ref[...] * rot
    sb = sb_ref[...] * rot
    h = h_ref[...]
    for blk in range(QKV_GROUP // MXU_COLS):
        acc = jnp.dot(h, w_ref[:, blk * MXU_COLS:(blk + 1) * MXU_COLS], preferred_element_type=F32)
        for half in range(MXU_COLS // LANES):
            ch = blk * (MXU_COLS // LANES) + half
            val = _rope_chunk(acc[:, half * LANES:(half + 1) * LANES], c, sa, sb, DIFF_HEAD_DIM // 2).astype(BF16)
            if ch < DIFF_WIDTH // LANES:
                od_ref[:, ch * LANES:(ch + 1) * LANES] = val
            else:
                r = ch - DIFF_WIDTH // LANES
                dst = (o0_ref, o1_ref, o2_ref)[r // 2]
                dst[:, (r % 2) * LANES:(r % 2 + 1) * LANES] = val


def _qkv_proj(x2, g, w, tabs, seq, tm):
    t = x2.shape[0]
    nrow = seq // tm
    tab_spec = pl.BlockSpec((tm, LANES), lambda i, j: (i % nrow, 0))
    dil_sd = jax.ShapeDtypeStruct((t, 3 * DIL_WIDTH), BF16)
    dil_spec = pl.BlockSpec((tm, DIL_WIDTH), lambda i, j: (i, j))
    return pl.pallas_call(
        functools.partial(_qkv_kernel, qscale=DIFF_HEAD_DIM ** -0.5 * LOG2E),
        out_shape=(jax.ShapeDtypeStruct((t, 3 * DIFF_WIDTH), BF16), dil_sd, dil_sd, dil_sd),
        grid=(t // tm, 3),
        in_specs=[pl.BlockSpec((tm, D_MODEL), lambda i, j: (i, 0)),
                  pl.BlockSpec((1, D_MODEL), lambda i, j: (0, 0)),
                  pl.BlockSpec((D_MODEL, QKV_GROUP), lambda i, j: (0, j)),
                  tab_spec, tab_spec, tab_spec],
        out_specs=(pl.BlockSpec((tm, DIFF_WIDTH), lambda i, j: (i, j)), dil_spec, dil_spec, dil_spec),
        scratch_shapes=[pltpu.VMEM((tm, D_MODEL), BF16)],
        compiler_params=_params(("parallel", "arbitrary")),
        name="qkv_proj",
    )(x2, g, w, *tabs)


def _gate_kernel(x_ref, g_ref, w_ref, b_ref, o_ref, h_ref):
    @pl.when(pl.program_id(1) == 0)
    def _():
        h_ref[...] = _rmsnorm(x_ref[...], g_ref[...]).astype(BF16)

    z = jnp.dot(h_ref[...], w_ref[...], preferred_element_type=F32) + b_ref[0]
    o_ref[...] = (1.0 / (1.0 + jnp.exp(-z))).astype(BF16)


def _gate_proj(x2, g, w, b, tm):
    t = x2.shape[0]
    return pl.pallas_call(
        _gate_kernel,
        out_shape=jax.ShapeDtypeStruct((t, 3 * D_MODEL), BF16),
        grid=(t // tm, 3),
        in_specs=[pl.BlockSpec((tm, D_MODEL), lambda i, j: (i, 0)),
                  pl.BlockSpec((1, D_MODEL), lambda i, j: (0, 0)),
                  pl.BlockSpec((D_MODEL, D_MODEL), lambda i, j: (0, j)),
                  pl.BlockSpec((1, 1, D_MODEL), lambda i, j: (j, 0, 0))],
        out_specs=pl.BlockSpec((tm, D_MODEL), lambda i, j: (i, j)),
        scratch_shapes=[pltpu.VMEM((tm, D_MODEL), BF16)],
        compiler_params=_params(("parallel", "arbitrary")),
        name="gate_proj",
    )(x2, g, w, b)


def _mla_pro_kernel(x_ref, g_ref, wcq_ref, wckv_ref, wkr_ref, gcq_ref, gckv_ref, wuq_ref, wuk_ref, wuv_ref,
                    c_ref, sa_ref, sb_ref, q_ref, k_ref, v_ref, *, qscale):
    h = _rmsnorm(x_ref[...], g_ref[...]).astype(BF16)
    cq = jnp.dot(h, wcq_ref[...], preferred_element_type=F32)
    cq = _rmsnorm(cq, gcq_ref[...]).astype(BF16)
    ckv = jnp.dot(h, wckv_ref[...], preferred_element_type=F32)
    ckv = _rmsnorm(ckv, gckv_ref[...]).astype(BF16)
    c, sa, sb = c_ref[...], sa_ref[...], sb_ref[...]
    half = MLA_ROPE_DIM // 2
    kr = _rope_chunk(jnp.dot(h, wkr_ref[...], preferred_element_type=F32), c, sa, sb, half)
    qh = jnp.dot(cq, wuq_ref[...], preferred_element_type=F32)
    kn = jnp.dot(ckv, wuk_ref[...], preferred_element_type=F32)
    cq_s, sa_s, sb_s = c * qscale, sa * qscale, sb * qscale
    for hd in range(MLA_HEADS):
        sl = slice(hd * MLA_PAD_DIM, (hd + 1) * MLA_PAD_DIM)
        q_ref[:, sl] = _rope_chunk(qh[:, sl], cq_s, sa_s, sb_s, half).astype(BF16)
        k_ref[:, sl] = (kn[:, sl] + kr).astype(BF16)
    v_ref[...] = jnp.dot(ckv, wuv_ref[...], preferred_element_type=F32).astype(BF16)


def _mla_prologue(x2, g, wcq, wckv, wkr, gcq, gckv, wuq, wuk, wuv, tabs, seq, tm):
    t = x2.shape[0]
    nrow = seq // tm
    full = lambda a: pl.BlockSpec(a.shape, lambda i: (0,) * a.ndim)
    tab_spec = pl.BlockSpec((tm, LANES), lambda i: (i % nrow, 0))
    hp = MLA_HEADS * MLA_PAD_DIM
    return pl.pallas_call(
        functools.partial(_mla_pro_kernel, qscale=(MLA_NOPE_DIM + MLA_ROPE_DIM) ** -0.5 * LOG2E),
        out_shape=(jax.ShapeDtypeStruct((t, hp), BF16), jax.ShapeDtypeStruct((t, hp), BF16),
                   jax.ShapeDtypeStruct((t, MLA_WIDTH), BF16)),
        grid=(t // tm,),
        in_specs=[pl.BlockSpec((tm, D_MODEL), lambda i: (i, 0)), full(g), full(wcq), full(wckv), full(wkr),
                  full(gcq), full(gckv), full(wuq), full(wuk), full(wuv), tab_spec, tab_spec, tab_spec],
        out_specs=(pl.BlockSpec((tm, hp), lambda i: (i, 0)), pl.BlockSpec((tm, hp), lambda i: (i, 0)),
                   pl.BlockSpec((tm, MLA_WIDTH), lambda i: (i, 0))),
        compiler_params=_params(("parallel",)),
        name="mla_prologue",
    )(x2, g, wcq, wckv, wkr, gcq, gckv, wuq, wuk, wuv, *tabs)


_NT = (((1,), (1,)), ((), ()))
N_STREAMS = 2


def _chunk(off, tk):
    return pl.ds(off if isinstance(off, int) else pl.multiple_of(off, tk), tk)


def _softmax_phase(st, slot):
    sbuf, pbuf, abuf, m_ref, l_ref, _ = st
    for z in range(N_STREAMS):
        s = sbuf[z, slot]
        tk = s.shape[1]
        m_prev = m_ref[z]
        m_new = jnp.maximum(m_prev, jnp.max(s, axis=1, keepdims=True))
        alpha = jnp.exp2(m_prev - m_new)
        p = jnp.exp2(s - jnp.concatenate([m_new] * (tk // LANES), axis=1))
        psum = p[:, :LANES]
        for c in range(1, tk // LANES):
            psum = psum + p[:, c * LANES:(c + 1) * LANES]
        m_ref[z] = m_new
        l_ref[z] = alpha * l_ref[z] + psum
        abuf[z, slot] = alpha
        pbuf[z, slot] = p.astype(BF16)


def _pv_phase(st, v, slot):
    _, pbuf, abuf, _, _, a_ref = st
    for z in range(N_STREAMS):
        a_ref[z] = abuf[z, slot] * a_ref[z] + jnp.dot(pbuf[z, slot], v, preferred_element_type=F32)


def _attend(qk_phase, v_of, nk, tk, st):
    assert nk % 2 == 0 and nk >= 2
    sbuf, pbuf, abuf, m_ref, l_ref, a_ref = st
    m_ref[...] = jnp.full(m_ref.shape, NEG_BIG, F32)
    l_ref[...] = jnp.zeros(l_ref.shape, F32)
    a_ref[...] = jnp.zeros(a_ref.shape, F32)
    qk_phase(0, 0)
    for i in range(nk):
        if i + 1 < nk:
            qk_phase((i + 1) * tk, (i + 1) % 2)
        if i > 0:
            _pv_phase(st, v_of((i - 1) * tk), (i - 1) % 2)
        _softmax_phase(st, i % 2)
    _pv_phase(st, v_of((nk - 1) * tk), (nk - 1) % 2)


def _normalised(st, z):
    return st[5][z] / jnp.sum(st[4][z], axis=1, keepdims=True)


Q_TILES = 2
N_STATE = 6


def _attn_scratch(tq, tk):
    one = [pltpu.VMEM((N_STREAMS, 2, tq, tk), F32), pltpu.VMEM((N_STREAMS, 2, tq, tk), BF16),
           pltpu.VMEM((N_STREAMS, 2, tq, LANES), F32)] + [pltpu.VMEM((N_STREAMS, tq, LANES), F32)] * 3
    return one * Q_TILES


def _diff_attn_kernel(lam_ref, gd_ref, q_ref, k_ref, v_ref, o_ref, *scratch, tq, tk, lam_init):
    t = lam_ref[...]
    lam = (jnp.exp(jnp.sum(t[0:1] * t[1:2], axis=1, keepdims=True))
           - jnp.exp(jnp.sum(t[2:3] * t[3:4], axis=1, keepdims=True)) + lam_init)
    lane = lax.broadcasted_iota(jnp.int32, (tq, LANES), 1)
    for qt in range(Q_TILES):
        st = scratch[qt * N_STATE:(qt + 1) * N_STATE]
        q = q_ref[0, qt * tq:(qt + 1) * tq, :]
        zero = jnp.zeros_like(q)
        q12 = jnp.concatenate([jnp.where(lane < DIFF_HEAD_DIM, q, zero),
                               jnp.where(lane >= DIFF_HEAD_DIM, q, zero)], axis=0)

        def qk_phase(off, slot, q12=q12, st=st):
            s = lax.dot_general(q12, k_ref[0, _chunk(off, tk), :], _NT, preferred_element_type=F32)
            st[0][0, slot] = s[:tq]
            st[0][1, slot] = s[tq:]

        _attend(qk_phase, lambda off: v_ref[0, _chunk(off, tk), :], k_ref.shape[1] // tk, tk, st)
        o = _normalised(st, 0) - lam * _normalised(st, 1)
        o_ref[0, qt * tq:(qt + 1) * tq, :] = (_rmsnorm(o, gd_ref[...]) * (1.0 - lam_init)).astype(BF16)


def _diff_attention(qkv, lam_p, g_diff, layer_idx, tq, tk):
    b, s, _ = qkv.shape
    kcol, vcol = DIFF_WIDTH // LANES, 2 * DIFF_WIDTH // LANES
    lam_init = 0.8 - 0.6 * math.exp(-0.3 * layer_idx)
    return pl.pallas_call(
        functools.partial(_diff_attn_kernel, tq=tq, tk=tk, lam_init=lam_init),
        out_shape=jax.ShapeDtypeStruct((b, s, DIFF_WIDTH), BF16),
        grid=(b, DIFF_HEADS, s // (Q_TILES * tq)),
        in_specs=[pl.BlockSpec((4, DIFF_HEAD_DIM), lambda bi, h, qi: (0, 0)),
                  pl.BlockSpec((1, DIFF_V_DIM), lambda bi, h, qi: (0, 0)),
                  pl.BlockSpec((1, Q_TILES * tq, LANES), lambda bi, h, qi: (bi, qi, h)),
                  pl.BlockSpec((1, s, LANES), lambda bi, h, qi: (bi, 0, kcol + h)),
                  pl.BlockSpec((1, s, LANES), lambda bi, h, qi: (bi, 0, vcol + h))],
        out_specs=pl.BlockSpec((1, Q_TILES * tq, LANES), lambda bi, h, qi: (bi, qi, h)),
        scratch_shapes=_attn_scratch(tq, tk),
        compiler_params=_params(("parallel", "parallel", "arbitrary")),
        name="diff_attention",
    )(lam_p, g_diff, qkv, qkv, qkv)


def _mla_attn_kernel(q_ref, k_ref, v_ref, o_ref, *scratch, tq, tk):
    lane = lax.broadcasted_iota(jnp.int32, (tq, LANES), 1)
    for qt in range(Q_TILES):
        st = scratch[qt * N_STATE:(qt + 1) * N_STATE]
        rows = slice(qt * tq, (qt + 1) * tq)
        qs = (q_ref[0, rows, :MLA_PAD_DIM], q_ref[0, rows, MLA_PAD_DIM:])

        def qk_phase(off, slot, qs=qs, st=st):
            for z in range(N_STREAMS):
                kz = k_ref[0, _chunk(off, tk), z * MLA_PAD_DIM:(z + 1) * MLA_PAD_DIM]
                st[0][z, slot] = lax.dot_general(qs[z], kz, _NT, preferred_element_type=F32)

        _attend(qk_phase, lambda off: v_ref[0, _chunk(off, tk), :], k_ref.shape[1] // tk, tk, st)
        o_ref[0, rows, :] = jnp.where(lane < MLA_V_DIM, _normalised(st, 0), _normalised(st, 1)).astype(BF16)


def _mla_attention(q, k, v, tq, tk):
    b, s, _ = q.shape
    return pl.pallas_call(
        functools.partial(_mla_attn_kernel, tq=tq, tk=tk),
        out_shape=jax.ShapeDtypeStruct((b, s, MLA_WIDTH), BF16),
        grid=(b, MLA_HEADS // 2, s // (Q_TILES * tq)),
        in_specs=[pl.BlockSpec((1, Q_TILES * tq, 2 * MLA_PAD_DIM), lambda bi, h, qi: (bi, qi, h)),
                  pl.BlockSpec((1, s, 2 * MLA_PAD_DIM), lambda bi, h, qi: (bi, 0, h)),
                  pl.BlockSpec((1, s, LANES), lambda bi, h, qi: (bi, 0, h))],
        out_specs=pl.BlockSpec((1, Q_TILES * tq, LANES), lambda bi, h, qi: (bi, qi, h)),
        scratch_shapes=_attn_scratch(tq, tk),
        compiler_params=_params(("parallel", "parallel", "arbitrary")),
        name="mla_attention",
    )(q, k, v)


def _dil_attn_kernel(q_ref, k_ref, v_ref, o_ref, lse_ref, *, tq, win):
    sub_len = k_ref.shape[1]
    lane = lax.broadcasted_iota(jnp.int32, (tq, DIL_WIDTH), 1)
    heads = [(lane // DIL_HEAD_DIM) == h for h in range(DIL_HEADS)]
    row = lax.broadcasted_iota(jnp.int32, (DIL_HEADS * tq, win), 0) % tq
    col = lax.broadcasted_iota(jnp.int32, (DIL_HEADS * tq, win), 1)
    for t in range(q_ref.shape[1] // tq):
        r0 = pl.program_id(2) * q_ref.shape[1] + t * tq
        start = pl.multiple_of(jnp.clip(r0 - DIL_SPAN, 0, sub_len - win), DIL_SPAN)
        q = q_ref[0, t * tq:(t + 1) * tq, :]
        kw = k_ref[0, pl.ds(start, win), :]
        vw = v_ref[0, pl.ds(start, win), :]
        band = jnp.abs(row - col + (r0 - start)) <= DIL_SPAN
        qz = jnp.concatenate([jnp.where(hm, q, jnp.zeros_like(q)) for hm in heads], axis=0)
        s = jnp.where(band, lax.dot_general(qz, kw, _NT, preferred_element_type=F32), NEG_BIG)
        m = jnp.max(s, axis=1, keepdims=True)
        p = jnp.exp2(s - m)
        l = jnp.sum(p, axis=1, keepdims=True)
        on = jnp.dot(p.astype(BF16), vw, preferred_element_type=F32) / l
        ls = m * LN2 + jnp.log(l)
        o = jnp.zeros(q.shape, F32)
        lse = jnp.zeros(q.shape, F32)
        for h, hm in enumerate(heads):
            o = jnp.where(hm, on[h * tq:(h + 1) * tq], o)
            lse = jnp.where(hm, ls[h * tq:(h + 1) * tq], lse)
        o_ref[0, t * tq:(t + 1) * tq, :] = o
        lse_ref[0, t * tq:(t + 1) * tq, :] = lse


def _dil_attention(qkv_g, dilation, group):
    b, s, _ = qkv_g.shape
    sub_len = s // dilation
    tq = min(128, sub_len)
    win = min(tq + 2 * DIL_SPAN, sub_len)
    rows = min(4 * tq, sub_len)
    sub = qkv_g.reshape(b, sub_len, dilation * 3 * DIL_WIDTH)
    out_sd = jax.ShapeDtypeStruct((b, sub_len, dilation * DIL_WIDTH), F32)
    o, lse = pl.pallas_call(
        functools.partial(_dil_attn_kernel, tq=tq, win=win),
        out_shape=(out_sd, out_sd),
        grid=(b, dilation, sub_len // rows),
        in_specs=[pl.BlockSpec((1, rows, DIL_WIDTH), lambda bi, c, qi: (bi, qi, 3 * c)),
                  pl.BlockSpec((1, sub_len, DIL_WIDTH), lambda bi, c, qi: (bi, 0, 3 * c + 1)),
                  pl.BlockSpec((1, sub_len, DIL_WIDTH), lambda bi, c, qi: (bi, 0, 3 * c + 2))],
        out_specs=(pl.BlockSpec((1, rows, DIL_WIDTH), lambda bi, c, qi: (bi, qi, c)),
                   pl.BlockSpec((1, rows, DIL_WIDTH), lambda bi, c, qi: (bi, qi, c))),
        compiler_params=_params(("parallel", "parallel", "arbitrary")),
        name="dil_attention_%d" % group,
    )(sub, sub, sub)
    return o.reshape(b * s, DIL_WIDTH), lse.reshape(b * s, DIL_WIDTH)


def _merge_kernel(x_ref, oa_ref, oc_ref, o0_ref, o1_ref, o2_ref, e0_ref, e1_ref, e2_ref, gt_ref,
                  wa_ref, wb_ref, wc_ref, wo_ref, y_ref):
    e0, e1, e2 = e0_ref[...], e1_ref[...], e2_ref[...]
    top = jnp.maximum(jnp.maximum(e0, e1), e2)
    p0, p1, p2 = jnp.exp(e0 - top), jnp.exp(e1 - top), jnp.exp(e2 - top)
    ob = (p0 * o0_ref[...] + p1 * o1_ref[...] + p2 * o2_ref[...]) / (p0 + p1 + p2)
    ya = jnp.dot(oa_ref[...], wa_ref[...], preferred_element_type=F32)
    yb = jnp.dot(ob.astype(BF16), wb_ref[...], preferred_element_type=F32)
    yc = jnp.dot(oc_ref[...], wc_ref[...], preferred_element_type=F32)
    merged = (gt_ref[:, :D_MODEL].astype(F32) * ya + gt_ref[:, D_MODEL:2 * D_MODEL].astype(F32) * yb
              + gt_ref[:, 2 * D_MODEL:].astype(F32) * yc)
    y_ref[...] = x_ref[...] + jnp.dot(merged.astype(BF16), wo_ref[...], preferred_element_type=F32)


def _merge_out(x2, oa, oc, dil, gates, wa, wb, wc, wo, tm):
    t = x2.shape[0]
    row = lambda w: pl.BlockSpec((tm, w), lambda i: (i, 0))
    full = lambda a: pl.BlockSpec(a.shape, lambda i: (0, 0))
    (o0, e0), (o1, e1), (o2, e2) = dil
    return pl.pallas_call(
        _merge_kernel,
        out_shape=jax.ShapeDtypeStruct((t, D_MODEL), F32),
        grid=(t // tm,),
        in_specs=[row(D_MODEL), row(DIFF_WIDTH), row(MLA_WIDTH)] + [row(DIL_WIDTH)] * 6 + [row(3 * D_MODEL)]
                 + [full(wa), full(wb), full(wc), full(wo)],
        out_specs=row(D_MODEL),
        compiler_params=_params(("parallel",)),
        name="merge_out",
    )(x2, oa, oc, o0, o1, o2, e0, e1, e2, gates, wa, wb, wc, wo)


def _mlp_kernel(x_ref, g_ref, wu_ref, wd_ref, gf_ref, y_ref, h_ref, acc_ref, *, final_norm):
    j = pl.program_id(1)

    @pl.when(j == 0)
    def _():
        h_ref[...] = _rmsnorm(x_ref[...], g_ref[...]).astype(BF16)
        acc_ref[...] = x_ref[...]

    u = jnp.dot(h_ref[...], wu_ref[...], preferred_element_type=F32)
    u = jnp.square(jnp.maximum(u, 0.0)).astype(BF16)
    acc_ref[...] += jnp.dot(u, wd_ref[...], preferred_element_type=F32)

    @pl.when(j == pl.num_programs(1) - 1)
    def _():
        if final_norm:
            y_ref[...] = _rmsnorm(acc_ref[...], gf_ref[...])
        else:
            y_ref[...] = acc_ref[...]


def _mlp(x2, g, wu, wd, g_final, final_norm, tm, th):
    t = x2.shape[0]
    return pl.pallas_call(
        functools.partial(_mlp_kernel, final_norm=final_norm),
        out_shape=jax.ShapeDtypeStruct((t, D_MODEL), F32),
        grid=(t // tm, MLP_HIDDEN // th),
        in_specs=[pl.BlockSpec((tm, D_MODEL), lambda i, j: (i, 0)),
                  pl.BlockSpec((1, D_MODEL), lambda i, j: (0, 0)),
                  pl.BlockSpec((D_MODEL, th), lambda i, j: (0, j)),
                  pl.BlockSpec((th, D_MODEL), lambda i, j: (j, 0)),
                  pl.BlockSpec((1, D_MODEL), lambda i, j: (0, 0))],
        out_specs=pl.BlockSpec((tm, D_MODEL), lambda i, j: (i, 0)),
        scratch_shapes=[pltpu.VMEM((tm, D_MODEL), BF16), pltpu.VMEM((tm, D_MODEL), F32)],
        compiler_params=_params(("parallel", "arbitrary")),
        name="mlp",
    )(x2, g, wu, wd, g_final)


def _rope_tables(seq, dim, first_lane, period):
    half = dim // 2
    inv = ROPE_THETA ** (-jnp.arange(half, dtype=F32) / half)
    ang = jnp.arange(seq, dtype=F32)[:, None] * inv[None, :]
    lane = jnp.arange(LANES)
    rel = (lane % period) - first_lane
    active = (rel >= 0) & (rel < dim)
    idx = jnp.clip(rel, 0, dim - 1) % half
    cos = jnp.where(active[None, :], jnp.cos(ang)[:, idx], 1.0)
    sin = jnp.where(active[None, :], jnp.sin(ang)[:, idx], 0.0)
    first = (rel < half)[None, :]
    return cos, jnp.where(first, -sin, 0.0), jnp.where(first, 0.0, sin)


def _layer_weights(w_in, w_uq, w_ukv):
    def cols(a, n):
        return w_in[:, a:a + n]
    dq, dk, dv = cols(0, 512), cols(512, 512), cols(1024, 512)
    dil = [[cols(1536 + 768 * g + 256 * t, 256) for t in range(3)] for g in range(3)]
    w_qkv = jnp.concatenate([dq] + [dil[g][0] for g in range(3)] + [dk] + [dil[g][1] for g in range(3)]
                            + [dv] + [dil[g][2] for g in range(3)], axis=1).astype(BF16)
    base = 3840
    w_cq = cols(base, MLA_Q_LORA).astype(BF16)
    w_ckv = cols(base + MLA_Q_LORA, MLA_KV_LORA).astype(BF16)
    w_kr = cols(base + MLA_Q_LORA + MLA_KV_LORA, MLA_ROPE_DIM)
    w_kr = jnp.pad(w_kr, ((0, 0), (MLA_NOPE_DIM, MLA_PAD_DIM - MLA_NOPE_DIM - MLA_ROPE_DIM))).astype(BF16)
    w_gate = cols(base + MLA_Q_LORA + MLA_KV_LORA + MLA_ROPE_DIM, 3 * D_MODEL).astype(BF16)
    qd = MLA_NOPE_DIM + MLA_ROPE_DIM
    w_uq_p = jnp.pad(w_uq.reshape(MLA_Q_LORA, MLA_HEADS, qd), ((0, 0), (0, 0), (0, MLA_PAD_DIM - qd)))
    w_uq_p = w_uq_p.reshape(MLA_Q_LORA, MLA_HEADS * MLA_PAD_DIM).astype(BF16)
    kv = w_ukv.reshape(MLA_KV_LORA, MLA_HEADS, MLA_NOPE_DIM + MLA_V_DIM)
    w_uk_p = jnp.pad(kv[:, :, :MLA_NOPE_DIM], ((0, 0), (0, 0), (0, MLA_PAD_DIM - MLA_NOPE_DIM)))
    w_uk_p = w_uk_p.reshape(MLA_KV_LORA, MLA_HEADS * MLA_PAD_DIM).astype(BF16)
    w_uv = kv[:, :, MLA_NOPE_DIM:].reshape(MLA_KV_LORA, MLA_WIDTH).astype(BF16)
    return w_qkv, w_cq, w_ckv, w_kr, w_gate, w_uq_p, w_uk_p, w_uv


def kernel(x, w_in, b_gate, g_mix, diff_lambda, g_diff, g_cq, g_ckv, w_uq, w_ukv, w_o_diff, w_o_dil, w_o_mla,
           w_out, g_mlp, w_up, w_down, g_final):
    b, s, d = x.shape
    assert d == D_MODEL and s % 1024 == 0
    depth = w_in.shape[0]
    tm = min(1024, s)
    tq, tk = 256, min(512, s)
    tabs64 = _rope_tables(s, DIFF_HEAD_DIM, 0, DIFF_HEAD_DIM)
    tabs_mla = _rope_tables(s, MLA_ROPE_DIM, MLA_NOPE_DIM, MLA_PAD_DIM)
    x2 = x.reshape(b * s, d)
    for l in range(depth):
        w_qkv, w_cq, w_ckv, w_kr, w_gate, w_uq_p, w_uk_p, w_uv = _layer_weights(w_in[l], w_uq[l], w_ukv[l])
        g = g_mix[l][None, :]
        qkv_diff, *qkv_dil = _qkv_proj(x2, g, w_qkv, tabs64, s, tm)
        gates = _gate_proj(x2, g, w_gate, b_gate[l][:, None, :], tm)
        q_c, k_c, v_c = _mla_prologue(x2, g, w_cq, w_ckv, w_kr, g_cq[l][None, :], g_ckv[l][None, :],
                                      w_uq_p, w_uk_p, w_uv, tabs_mla, s, min(512, tm))
        o_a = _diff_attention(qkv_diff.reshape(b, s, 3 * DIFF_WIDTH), diff_lambda[l], g_diff[l][None, :], l, tq, tk)
        dil = [_dil_attention(qkv_dil[gi].reshape(b, s, 3 * DIL_WIDTH), dilation, gi)
               for gi, (_, dilation) in enumerate(DIL_PATTERNS)]
        hp = MLA_HEADS * MLA_PAD_DIM
        o_c = _mla_attention(q_c.reshape(b, s, hp), k_c.reshape(b, s, hp), v_c.reshape(b, s, MLA_WIDTH), tq, tk)
        x2 = _merge_out(x2, o_a.reshape(b * s, DIFF_WIDTH), o_c.reshape(b * s, MLA_WIDTH), dil, gates,
                        w_o_diff[l].astype(BF16), w_o_dil[l].astype(BF16), w_o_mla[l].astype(BF16),
                        w_out[l].astype(BF16), min(512, tm))
        x2 = _mlp(x2, g_mlp[l][None, :], w_up[l].astype(BF16), w_down[l].astype(BF16), g_final[None, :],
                  l == depth - 1, tm, 1024)
    return x2.reshape(b, s, d)
```

```python
import functools
import math

import jax
import jax.numpy as jnp
from jax import lax
from jax.experimental import pallas as pl
from jax.experimental.pallas import tpu as pltpu

F32 = jnp.float32
BF16 = jnp.bfloat16

D_MODEL = 1024
ROPE_THETA = 10000.0
NORM_EPS = 1e-6
NEG_BIG = -1e30
LOG2E = math.log2(math.e)
LN2 = math.log(2.0)

DIFF_HEADS = 4
DIFF_HEAD_DIM = 64
DIFF_V_DIM = 128
DIFF_WIDTH = 512

DIL_PATTERNS = ((128, 1), (512, 4), (2048, 16))
DIL_SPAN = 64
DIL_HEADS = 4
DIL_HEAD_DIM = 64
DIL_WIDTH = 256

MLA_HEADS = 8
MLA_NOPE_DIM = 64
MLA_ROPE_DIM = 32
MLA_V_DIM = 64
MLA_Q_LORA = 768
MLA_KV_LORA = 256
MLA_WIDTH = 512
MLA_PAD_DIM = 128

MLP_HIDDEN = 4096
LANES = 128
MXU_COLS = 256

QKV_GROUP = 1280
VMEM_LIMIT = 56 * 1024 * 1024


def _params(semantics):
    return pltpu.CompilerParams(dimension_semantics=semantics, vmem_limit_bytes=VMEM_LIMIT)


def _rmsnorm(x, g):
    ms = jnp.mean(x * x, axis=-1, keepdims=True)
    return x * lax.rsqrt(ms + NORM_EPS) * g


def _rope_chunk(a, c, sa, sb, half):
    return a * c + pltpu.roll(a, LANES - half, 1) * sa + pltpu.roll(a, half, 1) * sb


def _qkv_kernel(x_ref, g_ref, w_ref, c_ref, sa_ref, sb_ref, od_ref, o0_ref, o1_ref, o2_ref, h_ref, *, qscale):
    j = pl.program_id(1)

    @pl.when(j == 0)
    def _():
        h_ref[...] = _rmsnorm(x_ref[...], g_ref[...]).astype(BF16)

    def project(epilogue):
        h = h_ref[...]
        for blk in range(QKV_GROUP // MXU_COLS):
            acc = jnp.dot(h, w_ref[:, blk * MXU_COLS:(blk + 1) * MXU_COLS], preferred_element_type=F32)
            for half in range(MXU_COLS // LANES):
                ch = blk * (MXU_COLS // LANES) + half
                val = epilogue(acc[:, half * LANES:(half + 1) * LANES]).astype(BF16)
                if ch < DIFF_WIDTH // LANES:
                    od_ref[:, ch * LANES:(ch + 1) * LANES] = val
                else:
                    r = ch - DIFF_WIDTH // LANES
                    dst = (o0_ref, o1_ref, o2_ref)[r // 2]
                    dst[:, (r % 2) * LANES:(r % 2 + 1) * LANES] = val

    @pl.when(j < 2)
    def _():
        scale = jnp.where(j == 0, qscale, 1.0).astype(F32)
        c = c_ref[...] * scale
        sa = sa_ref[...] * scale
        sb = sb_ref[...] * scale
        project(lambda a: _rope_chunk(a, c, sa, sb, DIFF_HEAD_DIM // 2))

    @pl.when(j == 2)
    def _():
        project(lambda a: a)


def _qkv_proj(x2, g, w, tabs, seq, tm):
    t = x2.shape[0]
    nrow = seq // tm
    tab_spec = pl.BlockSpec((tm, LANES), lambda i, j: (i % nrow, 0))
    dil_sd = jax.ShapeDtypeStruct((t, 3 * DIL_WIDTH), BF16)
    dil_spec = pl.BlockSpec((tm, DIL_WIDTH), lambda i, j: (i, j))
    return pl.pallas_call(
        functools.partial(_qkv_kernel, qscale=DIFF_HEAD_DIM ** -0.5 * LOG2E),
        out_shape=(jax.ShapeDtypeStruct((t, 3 * DIFF_WIDTH), BF16), dil_sd, dil_sd, dil_sd),
        grid=(t // tm, 3),
        in_specs=[pl.BlockSpec((tm, D_MODEL), lambda i, j: (i, 0)),
                  pl.BlockSpec((1, D_MODEL), lambda i, j: (0, 0)),
                  pl.BlockSpec((D_MODEL, QKV_GROUP), lambda i, j: (0, j)),
                  tab_spec, tab_spec, tab_spec],
        out_specs=(pl.BlockSpec((tm, DIFF_WIDTH), lambda i, j: (i, j)), dil_spec, dil_spec, dil_spec),
        scratch_shapes=[pltpu.VMEM((tm, D_MODEL), BF16)],
        compiler_params=_params(("parallel", "arbitrary")),
        name="qkv_proj",
    )(x2, g, w, *tabs)


def _gate_kernel(x_ref, g_ref, w_ref, b_ref, o_ref, h_ref):
    @pl.when(pl.program_id(1) == 0)
    def _():
        h_ref[...] = _rmsnorm(x_ref[...], g_ref[...]).astype(BF16)

    z = jnp.dot(h_ref[...], w_ref[...], preferred_element_type=F32) + b_ref[0]
    o_ref[...] = (1.0 / (1.0 + jnp.exp(-z))).astype(BF16)


def _gate_proj(x2, g, w, b, tm):
    t = x2.shape[0]
    return pl.pallas_call(
        _gate_kernel,
        out_shape=jax.ShapeDtypeStruct((t, 3 * D_MODEL), BF16),
        grid=(t // tm, 3),
        in_specs=[pl.BlockSpec((tm, D_MODEL), lambda i, j: (i, 0)),
                  pl.BlockSpec((1, D_MODEL), lambda i, j: (0, 0)),
                  pl.BlockSpec((D_MODEL, D_MODEL), lambda i, j: (0, j)),
                  pl.BlockSpec((1, 1, D_MODEL), lambda i, j: (j, 0, 0))],
        out_specs=pl.BlockSpec((tm, D_MODEL), lambda i, j: (i, j)),
        scratch_shapes=[pltpu.VMEM((tm, D_MODEL), BF16)],
        compiler_params=_params(("parallel", "arbitrary")),
        name="gate_proj",
    )(x2, g, w, b)


def _mla_pro_kernel(x_ref, g_ref, wcq_ref, wckv_ref, wkr_ref, gcq_ref, gckv_ref, wuq_ref, wuk_ref, wuv_ref,
                    c_ref, sa_ref, sb_ref, q_ref, k_ref, v_ref, *, qscale):
    h = _rmsnorm(x_ref[...], g_ref[...]).astype(BF16)
    cq = jnp.dot(h, wcq_ref[...], preferred_element_type=F32)
    cq = _rmsnorm(cq, gcq_ref[...]).astype(BF16)
    ckv = jnp.dot(h, wckv_ref[...], preferred_element_type=F32)
    ckv = _rmsnorm(ckv, gckv_ref[...]).astype(BF16)
    c, sa, sb = c_ref[...], sa_ref[...], sb_ref[...]
    half = MLA_ROPE_DIM // 2
    kr = _rope_chunk(jnp.dot(h, wkr_ref[...], preferred_element_type=F32), c, sa, sb, half)
    qh = jnp.dot(cq, wuq_ref[...], preferred_element_type=F32)
    kn = jnp.dot(ckv, wuk_ref[...], preferred_element_type=F32)
    cq_s, sa_s, sb_s = c * qscale, sa * qscale, sb * qscale
    for hd in range(MLA_HEADS):
        sl = slice(hd * MLA_PAD_DIM, (hd + 1) * MLA_PAD_DIM)
        q_ref[:, sl] = _rope_chunk(qh[:, sl], cq_s, sa_s, sb_s, half).astype(BF16)
        k_ref[:, sl] = (kn[:, sl] + kr).astype(BF16)
    v_ref[...] = jnp.dot(ckv, wuv_ref[...], preferred_element_type=F32).astype(BF16)


def _mla_prologue(x2, g, wcq, wckv, wkr, gcq, gckv, wuq, wuk, wuv, tabs, seq, tm):
    t = x2.shape[0]
    nrow = seq // tm
    full = lambda a: pl.BlockSpec(a.shape, lambda i: (0,) * a.ndim)
    tab_spec = pl.BlockSpec((tm, LANES), lambda i: (i % nrow, 0))
    hp = MLA_HEADS * MLA_PAD_DIM
    return pl.pallas_call(
        functools.partial(_mla_pro_kernel, qscale=(MLA_NOPE_DIM + MLA_ROPE_DIM) ** -0.5 * LOG2E),
        out_shape=(jax.ShapeDtypeStruct((t, hp), BF16), jax.ShapeDtypeStruct((t, hp), BF16),
                   jax.ShapeDtypeStruct((t, MLA_WIDTH), BF16)),
        grid=(t // tm,),
        in_specs=[pl.BlockSpec((tm, D_MODEL), lambda i: (i, 0)), full(g), full(wcq), full(wckv), full(wkr),
                  full(gcq), full(gckv), full(wuq), full(wuk), full(wuv), tab_spec, tab_spec, tab_spec],
        out_specs=(pl.BlockSpec((tm, hp), lambda i: (i, 0)), pl.BlockSpec((tm, hp), lambda i: (i, 0)),
                   pl.BlockSpec((tm, MLA_WIDTH), lambda i: (i, 0))),
        compiler_params=_params(("parallel",)),
        name="mla_prologue",
    )(x2, g, wcq, wckv, wkr, gcq, gckv, wuq, wuk, wuv, *tabs)


_NT = (((1,), (1,)), ((), ()))
N_STREAMS = 2


def _chunk(off, tk):
    return pl.ds(off if isinstance(off, int) else pl.multiple_of(off, tk), tk)


def _softmax_phase(st, slot):
    sbuf, pbuf, abuf, m_ref, l_ref, _ = st
    for z in range(N_STREAMS):
        s = sbuf[z, slot]
        tk = s.shape[1]
        m_prev = m_ref[z]
        m_new = jnp.maximum(m_prev, jnp.max(s, axis=1, keepdims=True))
        alpha = jnp.exp2(m_prev - m_new)
        p = jnp.exp2(s - jnp.concatenate([m_new] * (tk // LANES), axis=1))
        psum = p[:, :LANES]
        for c in range(1, tk // LANES):
            psum = psum + p[:, c * LANES:(c + 1) * LANES]
        m_ref[z] = m_new
        l_ref[z] = alpha * l_ref[z] + psum
        abuf[z, slot] = alpha
        pbuf[z, slot] = p.astype(BF16)


def _pv_phase(st, v, slot):
    _, pbuf, abuf, _, _, a_ref = st
    for z in range(N_STREAMS):
        a_ref[z] = abuf[z, slot] * a_ref[z] + jnp.dot(pbuf[z, slot], v, preferred_element_type=F32)


def _attend(qk_phase, v_of, nk, tk, st):
    assert nk % 2 == 0 and nk >= 2
    sbuf, pbuf, abuf, m_ref, l_ref, a_ref = st
    m_ref[...] = jnp.full(m_ref.shape, NEG_BIG, F32)
    l_ref[...] = jnp.zeros(l_ref.shape, F32)
    a_ref[...] = jnp.zeros(a_ref.shape, F32)
    qk_phase(0, 0)
    for i in range(nk):
        if i + 1 < nk:
            qk_phase((i + 1) * tk, (i + 1) % 2)
        if i > 0:
            _pv_phase(st, v_of((i - 1) * tk), (i - 1) % 2)
        _softmax_phase(st, i % 2)
    _pv_phase(st, v_of((nk - 1) * tk), (nk - 1) % 2)


def _normalised(st, z):
    return st[5][z] / jnp.sum(st[4][z], axis=1, keepdims=True)


Q_TILES = 2
N_STATE = 6


def _attn_scratch(tq, tk):
    one = [pltpu.VMEM((N_STREAMS, 2, tq, tk), F32), pltpu.VMEM((N_STREAMS, 2, tq, tk), BF16),
           pltpu.VMEM((N_STREAMS, 2, tq, LANES), F32)] + [pltpu.VMEM((N_STREAMS, tq, LANES), F32)] * 3
    return one * Q_TILES


def _diff_attn_kernel(lam_ref, gd_ref, q_ref, k_ref, v_ref, o_ref, *scratch, tq, tk, lam_init):
    t = lam_ref[...]
    lam = (jnp.exp(jnp.sum(t[0:1] * t[1:2], axis=1, keepdims=True))
           - jnp.exp(jnp.sum(t[2:3] * t[3:4], axis=1, keepdims=True)) + lam_init)
    lane = lax.broadcasted_iota(jnp.int32, (tq, LANES), 1)
    for qt in range(Q_TILES):
        st = scratch[qt * N_STATE:(qt + 1) * N_STATE]
        q = q_ref[0, qt * tq:(qt + 1) * tq, :]
        zero = jnp.zeros_like(q)
        q12 = jnp.concatenate([jnp.where(lane < DIFF_HEAD_DIM, q, zero),
                               jnp.where(lane >= DIFF_HEAD_DIM, q, zero)], axis=0)

        def qk_phase(off, slot, q12=q12, st=st):
            s = lax.dot_general(q12, k_ref[0, _chunk(off, tk), :], _NT, preferred_element_type=F32)
            st[0][0, slot] = s[:tq]
            st[0][1, slot] = s[tq:]

        _attend(qk_phase, lambda off: v_ref[0, _chunk(off, tk), :], k_ref.shape[1] // tk, tk, st)
        o = _normalised(st, 0) - lam * _normalised(st, 1)
        o_ref[0, qt * tq:(qt + 1) * tq, :] = (_rmsnorm(o, gd_ref[...]) * (1.0 - lam_init)).astype(BF16)


def _diff_attention(qkv, lam_p, g_diff, layer_idx, tq, tk):
    b, s, _ = qkv.shape
    kcol, vcol = DIFF_WIDTH // LANES, 2 * DIFF_WIDTH // LANES
    lam_init = 0.8 - 0.6 * math.exp(-0.3 * layer_idx)
    return pl.pallas_call(
        functools.partial(_diff_attn_kernel, tq=tq, tk=tk, lam_init=lam_init),
        out_shape=jax.ShapeDtypeStruct((b, s, DIFF_WIDTH), BF16),
        grid=(b, DIFF_HEADS, s // (Q_TILES * tq)),
        in_specs=[pl.BlockSpec((4, DIFF_HEAD_DIM), lambda bi, h, qi: (0, 0)),
                  pl.BlockSpec((1, DIFF_V_DIM), lambda bi, h, qi: (0, 0)),
                  pl.BlockSpec((1, Q_TILES * tq, LANES), lambda bi, h, qi: (bi, qi, h)),
                  pl.BlockSpec((1, s, LANES), lambda bi, h, qi: (bi, 0, kcol + h)),
                  pl.BlockSpec((1, s, LANES), lambda bi, h, qi: (bi, 0, vcol + h))],
        out_specs=pl.BlockSpec((1, Q_TILES * tq, LANES), lambda bi, h, qi: (bi, qi, h)),
        scratch_shapes=_attn_scratch(tq, tk),
        compiler_params=_params(("parallel", "parallel", "arbitrary")),
        name="diff_attention",
    )(lam_p, g_diff, qkv, qkv, qkv)


def _mla_attn_kernel(q_ref, k_ref, v_ref, o_ref, *scratch, tq, tk):
    lane = lax.broadcasted_iota(jnp.int32, (tq, LANES), 1)
    for qt in range(Q_TILES):
        st = scratch[qt * N_STATE:(qt + 1) * N_STATE]
        rows = slice(qt * tq, (qt + 1) * tq)
        qs = (q_ref[0, rows, :MLA_PAD_DIM], q_ref[0, rows, MLA_PAD_DIM:])

        def qk_phase(off, slot, qs=qs, st=st):
            for z in range(N_STREAMS):
                kz = k_ref[0, _chunk(off, tk), z * MLA_PAD_DIM:(z + 1) * MLA_PAD_DIM]
                st[0][z, slot] = lax.dot_general(qs[z], kz, _NT, preferred_element_type=F32)

        _attend(qk_phase, lambda off: v_ref[0, _chunk(off, tk), :], k_ref.shape[1] // tk, tk, st)
        o_ref[0, rows, :] = jnp.where(lane < MLA_V_DIM, _normalised(st, 0), _normalised(st, 1)).astype(BF16)


def _mla_attention(q, k, v, tq, tk):
    b, s, _ = q.shape
    return pl.pallas_call(
        functools.partial(_mla_attn_kernel, tq=tq, tk=tk),
        out_shape=jax.ShapeDtypeStruct((b, s, MLA_WIDTH), BF16),
        grid=(b, MLA_HEADS // 2, s // (Q_TILES * tq)),
        in_specs=[pl.BlockSpec((1, Q_TILES * tq, 2 * MLA_PAD_DIM), lambda bi, h, qi: (bi, qi, h)),
                  pl.BlockSpec((1, s, 2 * MLA_PAD_DIM), lambda bi, h, qi: (bi, 0, h)),
                  pl.BlockSpec((1, s, LANES), lambda bi, h, qi: (bi, 0, h))],
        out_specs=pl.BlockSpec((1, Q_TILES * tq, LANES), lambda bi, h, qi: (bi, qi, h)),
        scratch_shapes=_attn_scratch(tq, tk),
        compiler_params=_params(("parallel", "parallel", "arbitrary")),
        name="mla_attention",
    )(q, k, v)


def _dil_attn_kernel(q_ref, k_ref, v_ref, o_ref, lse_ref, *, tq, win):
    sub_len = k_ref.shape[1]
    lane = lax.broadcasted_iota(jnp.int32, (tq, DIL_WIDTH), 1)
    heads = [(lane // DIL_HEAD_DIM) == h for h in range(DIL_HEADS)]
    row = lax.broadcasted_iota(jnp.int32, (DIL_HEADS * tq, win), 0) % tq
    col = lax.broadcasted_iota(jnp.int32, (DIL_HEADS * tq, win), 1)
    for t in range(q_ref.shape[1] // tq):
        r0 = pl.program_id(2) * q_ref.shape[1] + t * tq
        start = pl.multiple_of(jnp.clip(r0 - DIL_SPAN, 0, sub_len - win), DIL_SPAN)
        q = q_ref[0, t * tq:(t + 1) * tq, :]
        kw = k_ref[0, pl.ds(start, win), :]
        vw = v_ref[0, pl.ds(start, win), :]
        band = jnp.abs(row - col + (r0 - start)) <= DIL_SPAN
        qz = jnp.concatenate([jnp.where(hm, q, jnp.zeros_like(q)) for hm in heads], axis=0)
        s = jnp.where(band, lax.dot_general(qz, kw, _NT, preferred_element_type=F32), NEG_BIG)
        m = jnp.max(s, axis=1, keepdims=True)
        p = jnp.exp2(s - m)
        l = jnp.sum(p, axis=1, keepdims=True)
        on = jnp.dot(p.astype(BF16), vw, preferred_element_type=F32) / l
        ls = m * LN2 + jnp.log(l)
        o = jnp.zeros(q.shape, F32)
        lse = jnp.zeros(q.shape, F32)
        for h, hm in enumerate(heads):
            o = jnp.where(hm, on[h * tq:(h + 1) * tq], o)
            lse = jnp.where(hm, ls[h * tq:(h + 1) * tq], lse)
        o_ref[0, t * tq:(t + 1) * tq, :] = o
        lse_ref[0, t * tq:(t + 1) * tq, :] = lse


def _dil_attention(qkv_g, dilation, group):
    b, s, _ = qkv_g.shape
    sub_len = s // dilation
    tq = min(128, sub_len)
    win = min(tq + 2 * DIL_SPAN, sub_len)
    rows = min(4 * tq, sub_len)
    sub = qkv_g.reshape(b, sub_len, dilation * 3 * DIL_WIDTH)
    out_sd = jax.ShapeDtypeStruct((b, sub_len, dilation * DIL_WIDTH), F32)
    o, lse = pl.pallas_call(
        functools.partial(_dil_attn_kernel, tq=tq, win=win),
        out_shape=(out_sd, out_sd),
        grid=(b, dilation, sub_len // rows),
        in_specs=[pl.BlockSpec((1, rows, DIL_WIDTH), lambda bi, c, qi: (bi, qi, 3 * c)),
                  pl.BlockSpec((1, sub_len, DIL_WIDTH), lambda bi, c, qi: (bi, 0, 3 * c + 1)),
                  pl.BlockSpec((1, sub_len, DIL_WIDTH), lambda bi, c, qi: (bi, 0, 3 * c + 2))],
        out_specs=(pl.BlockSpec((1, rows, DIL_WIDTH), lambda bi, c, qi: (bi, qi, c)),
                   pl.BlockSpec((1, rows, DIL_WIDTH), lambda bi, c, qi: (bi, qi, c))),
        compiler_params=_params(("parallel", "parallel", "arbitrary")),
        name="dil_attention_%d" % group,
    )(sub, sub, sub)
    return o.reshape(b * s, DIL_WIDTH), lse.reshape(b * s, DIL_WIDTH)


def _merge_kernel(x_ref, oa_ref, oc_ref, o0_ref, o1_ref, o2_ref, e0_ref, e1_ref, e2_ref, gt_ref,
                  wa_ref, wb_ref, wc_ref, wo_ref, y_ref):
    e0, e1, e2 = e0_ref[...], e1_ref[...], e2_ref[...]
    top = jnp.maximum(jnp.maximum(e0, e1), e2)
    p0, p1, p2 = jnp.exp(e0 - top), jnp.exp(e1 - top), jnp.exp(e2 - top)
    ob = (p0 * o0_ref[...] + p1 * o1_ref[...] + p2 * o2_ref[...]) / (p0 + p1 + p2)
    ya = jnp.dot(oa_ref[...], wa_ref[...], preferred_element_type=F32)
    yb = jnp.dot(ob.astype(BF16), wb_ref[...], preferred_element_type=F32)
    yc = jnp.dot(oc_ref[...], wc_ref[...], preferred_element_type=F32)
    merged = (gt_ref[:, :D_MODEL].astype(F32) * ya + gt_ref[:, D_MODEL:2 * D_MODEL].astype(F32) * yb
              + gt_ref[:, 2 * D_MODEL:].astype(F32) * yc)
    y_ref[...] = x_ref[...] + jnp.dot(merged.astype(BF16), wo_ref[...], preferred_element_type=F32)


def _merge_out(x2, oa, oc, dil, gates, wa, wb, wc, wo, tm):
    t = x2.shape[0]
    row = lambda w: pl.BlockSpec((tm, w), lambda i: (i, 0))
    full = lambda a: pl.BlockSpec(a.shape, lambda i: (0, 0))
    (o0, e0), (o1, e1), (o2, e2) = dil
    return pl.pallas_call(
        _merge_kernel,
        out_shape=jax.ShapeDtypeStruct((t, D_MODEL), F32),
        grid=(t // tm,),
        in_specs=[row(D_MODEL), row(DIFF_WIDTH), row(MLA_WIDTH)] + [row(DIL_WIDTH)] * 6 + [row(3 * D_MODEL)]
                 + [full(wa), full(wb), full(wc), full(wo)],
        out_specs=row(D_MODEL),
        compiler_params=_params(("parallel",)),
        name="merge_out",
    )(x2, oa, oc, o0, o1, o2, e0, e1, e2, gates, wa, wb, wc, wo)


def _mlp_kernel(x_ref, g_ref, wu_ref, wd_ref, gf_ref, y_ref, h_ref, acc_ref, *, final_norm):
    j = pl.program_id(1)

    @pl.when(j == 0)
    def _():
        h_ref[...] = _rmsnorm(x_ref[...], g_ref[...]).astype(BF16)
        acc_ref[...] = x_ref[...]

    u = jnp.dot(h_ref[...], wu_ref[...], preferred_element_type=F32)
    u = jnp.square(jnp.maximum(u, 0.0)).astype(BF16)
    acc_ref[...] += jnp.dot(u, wd_ref[...], preferred_element_type=F32)

    @pl.when(j == pl.num_programs(1) - 1)
    def _():
        if final_norm:
            y_ref[...] = _rmsnorm(acc_ref[...], gf_ref[...])
        else:
            y_ref[...] = acc_ref[...]


def _mlp(x2, g, wu, wd, g_final, final_norm, tm, th):
    t = x2.shape[0]
    return pl.pallas_call(
        functools.partial(_mlp_kernel, final_norm=final_norm),
        out_shape=jax.ShapeDtypeStruct((t, D_MODEL), F32),
        grid=(t // tm, MLP_HIDDEN // th),
        in_specs=[pl.BlockSpec((tm, D_MODEL), lambda i, j: (i, 0)),
                  pl.BlockSpec((1, D_MODEL), lambda i, j: (0, 0)),
                  pl.BlockSpec((D_MODEL, th), lambda i, j: (0, j)),
                  pl.BlockSpec((th, D_MODEL), lambda i, j: (j, 0)),
                  pl.BlockSpec((1, D_MODEL), lambda i, j: (0, 0))],
        out_specs=pl.BlockSpec((tm, D_MODEL), lambda i, j: (i, 0)),
        scratch_shapes=[pltpu.VMEM((tm, D_MODEL), BF16), pltpu.VMEM((tm, D_MODEL), F32)],
        compiler_params=_params(("parallel", "arbitrary")),
        name="mlp",
    )(x2, g, wu, wd, g_final)


def _rope_tables(seq, dim, first_lane, period):
    half = dim // 2
    inv = ROPE_THETA ** (-jnp.arange(half, dtype=F32) / half)
    ang = jnp.arange(seq, dtype=F32)[:, None] * inv[None, :]
    lane = jnp.arange(LANES)
    rel = (lane % period) - first_lane
    active = (rel >= 0) & (rel < dim)
    idx = jnp.clip(rel, 0, dim - 1) % half
    cos = jnp.where(active[None, :], jnp.cos(ang)[:, idx], 1.0)
    sin = jnp.where(active[None, :], jnp.sin(ang)[:, idx], 0.0)
    first = (rel < half)[None, :]
    return cos, jnp.where(first, -sin, 0.0), jnp.where(first, 0.0, sin)


def _layer_weights(w_in, w_uq, w_ukv):
    def cols(a, n):
        return w_in[:, a:a + n]
    dq, dk, dv = cols(0, 512), cols(512, 512), cols(1024, 512)
    dil = [[cols(1536 + 768 * g + 256 * t, 256) for t in range(3)] for g in range(3)]
    w_qkv = jnp.concatenate([dq] + [dil[g][0] for g in range(3)] + [dk] + [dil[g][1] for g in range(3)]
                            + [dv] + [dil[g][2] for g in range(3)], axis=1).astype(BF16)
    base = 3840
    w_cq = cols(base, MLA_Q_LORA).astype(BF16)
    w_ckv = cols(base + MLA_Q_LORA, MLA_KV_LORA).astype(BF16)
    w_kr = cols(base + MLA_Q_LORA + MLA_KV_LORA, MLA_ROPE_DIM)
    w_kr = jnp.pad(w_kr, ((0, 0), (MLA_NOPE_DIM, MLA_PAD_DIM - MLA_NOPE_DIM - MLA_ROPE_DIM))).astype(BF16)
    w_gate = cols(base + MLA_Q_LORA + MLA_KV_LORA + MLA_ROPE_DIM, 3 * D_MODEL).astype(BF16)
    qd = MLA_NOPE_DIM + MLA_ROPE_DIM
    w_uq_p = jnp.pad(w_uq.reshape(MLA_Q_LORA, MLA_HEADS, qd), ((0, 0), (0, 0), (0, MLA_PAD_DIM - qd)))
    w_uq_p = w_uq_p.reshape(MLA_Q_LORA, MLA_HEADS * MLA_PAD_DIM).astype(BF16)
    kv = w_ukv.reshape(MLA_KV_LORA, MLA_HEADS, MLA_NOPE_DIM + MLA_V_DIM)
    w_uk_p = jnp.pad(kv[:, :, :MLA_NOPE_DIM], ((0, 0), (0, 0), (0, MLA_PAD_DIM - MLA_NOPE_DIM)))
    w_uk_p = w_uk_p.reshape(MLA_KV_LORA, MLA_HEADS * MLA_PAD_DIM).astype(BF16)
    w_uv = kv[:, :, MLA_NOPE_DIM:].reshape(MLA_KV_LORA, MLA_WIDTH).astype(BF16)
    return w_qkv, w_cq, w_ckv, w_kr, w_gate, w_uq_p, w_uk_p, w_uv


def kernel(x, w_in, b_gate, g_mix, diff_lambda, g_diff, g_cq, g_ckv, w_uq, w_ukv, w_o_diff, w_o_dil, w_o_mla,
           w_out, g_mlp, w_up, w_down, g_final):
    b, s, d = x.shape
    assert d == D_MODEL and s % 1024 == 0
    depth = w_in.shape[0]
    tm = min(1024, s)
    tq, tk = 256, min(512, s)
    tabs64 = _rope_tables(s, DIFF_HEAD_DIM, 0, DIFF_HEAD_DIM)
    tabs_mla = _rope_tables(s, MLA_ROPE_DIM, MLA_NOPE_DIM, MLA_PAD_DIM)
    x2 = x.reshape(b * s, d)
    for l in range(depth):
        w_qkv, w_cq, w_ckv, w_kr, w_gate, w_uq_p, w_uk_p, w_uv = _layer_weights(w_in[l], w_uq[l], w_ukv[l])
        g = g_mix[l][None, :]
        qkv_diff, *qkv_dil = _qkv_proj(x2, g, w_qkv, tabs64, s, tm)
        gates = _gate_proj(x2, g, w_gate, b_gate[l][:, None, :], tm)
        q_c, k_c, v_c = _mla_prologue(x2, g, w_cq, w_ckv, w_kr, g_cq[l][None, :], g_ckv[l][None, :],
                                      w_uq_p, w_uk_p, w_uv, tabs_mla, s, min(512, tm))
        o_a = _diff_attention(qkv_diff.reshape(b, s, 3 * DIFF_WIDTH), diff_lambda[l], g_diff[l][None, :], l, tq, tk)
        dil = [_dil_attention(qkv_dil[gi].reshape(b, s, 3 * DIL_WIDTH), dilation, gi)
               for gi, (_, dilation) in enumerate(DIL_PATTERNS)]
        hp = MLA_HEADS * MLA_PAD_DIM
        o_c = _mla_attention(q_c.reshape(b, s, hp), k_c.reshape(b, s, hp), v_c.reshape(b, s, MLA_WIDTH), tq, tk)
        x2 = _merge_out(x2, o_a.reshape(b * s, DIFF_WIDTH), o_c.reshape(b * s, MLA_WIDTH), dil, gates,
                        w_o_diff[l].astype(BF16), w_o_dil[l].astype(BF16), w_o_mla[l].astype(BF16),
                        w_out[l].astype(BF16), min(512, tm))
        x2 = _mlp(x2, g_mlp[l][None, :], w_up[l].astype(BF16), w_down[l].astype(BF16), g_final[None, :],
                  l == depth - 1, tm, 1024)
    return x2.reshape(b, s, d)
```

```python
import functools
import math

import jax
import jax.numpy as jnp
from jax import lax
from jax.experimental import pallas as pl
from jax.experimental.pallas import tpu as pltpu

F32 = jnp.float32
BF16 = jnp.bfloat16

D_MODEL = 1024
ROPE_THETA = 10000.0
NORM_EPS = 1e-6
NEG_BIG = -1e30
LOG2E = math.log2(math.e)
LN2 = math.log(2.0)

DIFF_HEADS = 4
DIFF_HEAD_DIM = 64
DIFF_V_DIM = 128
DIFF_WIDTH = 512

DIL_PATTERNS = ((128, 1), (512, 4), (2048, 16))
DIL_SPAN = 64
DIL_HEADS = 4
DIL_HEAD_DIM = 64
DIL_WIDTH = 256

MLA_HEADS = 8
MLA_NOPE_DIM = 64
MLA_ROPE_DIM = 32
MLA_V_DIM = 64
MLA_Q_LORA = 768
MLA_KV_LORA = 256
MLA_WIDTH = 512
MLA_PAD_DIM = 128

MLP_HIDDEN = 4096
LANES = 128
MXU_COLS = 256

QKV_GROUP = 1280
VMEM_LIMIT = 56 * 1024 * 1024


def _params(semantics):
    return pltpu.CompilerParams(dimension_semantics=semantics, vmem_limit_bytes=VMEM_LIMIT)


def _rmsnorm(x, g):
    ms = jnp.mean(x * x, axis=-1, keepdims=True)
    return x * lax.rsqrt(ms + NORM_EPS) * g


def _rope_chunk(a, c, sa, sb, half):
    return a * c + pltpu.roll(a, LANES - half, 1) * sa + pltpu.roll(a, half, 1) * sb


def _qkv_kernel(x_ref, g_ref, w_ref, c_ref, sa_ref, sb_ref, od_ref, o0_ref, o1_ref, o2_ref, h_ref, *, qscale):
    j = pl.program_id(1)

    @pl.when(j == 0)
    def _():
        h_ref[...] = _rmsnorm(x_ref[...], g_ref[...]).astype(BF16)

    def project(epilogue):
        h = h_ref[...]
        for blk in range(QKV_GROUP // MXU_COLS):
            acc = jnp.dot(h, w_ref[:, blk * MXU_COLS:(blk + 1) * MXU_COLS], preferred_element_type=F32)
            for half in range(MXU_COLS // LANES):
                ch = blk * (MXU_COLS // LANES) + half
                val = epilogue(acc[:, half * LANES:(half + 1) * LANES]).astype(BF16)
                if ch < DIFF_WIDTH // LANES:
                    od_ref[:, ch * LANES:(ch + 1) * LANES] = val
                else:
                    r = ch - DIFF_WIDTH // LANES
                    dst = (o0_ref, o1_ref, o2_ref)[r // 2]
                    dst[:, (r % 2) * LANES:(r % 2 + 1) * LANES] = val

    @pl.when(j < 2)
    def _():
        scale = jnp.where(j == 0, qscale, 1.0).astype(F32)
        c = c_ref[...] * scale
        sa = sa_ref[...] * scale
        sb = sb_ref[...] * scale
        project(lambda a: _rope_chunk(a, c, sa, sb, DIFF_HEAD_DIM // 2))

    @pl.when(j == 2)
    def _():
        project(lambda a: a)


def _qkv_proj(x2, g, w, tabs, seq, tm):
    t = x2.shape[0]
    nrow = seq // tm
    tab_spec = pl.BlockSpec((tm, LANES), lambda i, j: (i % nrow, 0))
    dil_sd = jax.ShapeDtypeStruct((t, 3 * DIL_WIDTH), BF16)
    dil_spec = pl.BlockSpec((tm, DIL_WIDTH), lambda i, j: (i, j))
    return pl.pallas_call(
        functools.partial(_qkv_kernel, qscale=DIFF_HEAD_DIM ** -0.5 * LOG2E),
        out_shape=(jax.ShapeDtypeStruct((t, 3 * DIFF_WIDTH), BF16), dil_sd, dil_sd, dil_sd),
        grid=(t // tm, 3),
        in_specs=[pl.BlockSpec((tm, D_MODEL), lambda i, j: (i, 0)),
                  pl.BlockSpec((1, D_MODEL), lambda i, j: (0, 0)),
                  pl.BlockSpec((D_MODEL, QKV_GROUP), lambda i, j: (0, j)),
                  tab_spec, tab_spec, tab_spec],
        out_specs=(pl.BlockSpec((tm, DIFF_WIDTH), lambda i, j: (i, j)), dil_spec, dil_spec, dil_spec),
        scratch_shapes=[pltpu.VMEM((tm, D_MODEL), BF16)],
        compiler_params=_params(("parallel", "arbitrary")),
        name="qkv_proj",
    )(x2, g, w, *tabs)


def _gate_kernel(x_ref, g_ref, w_ref, b_ref, o_ref, h_ref):
    @pl.when(pl.program_id(1) == 0)
    def _():
        h_ref[...] = _rmsnorm(x_ref[...], g_ref[...]).astype(BF16)

    z = jnp.dot(h_ref[...], w_ref[...], preferred_element_type=F32) + b_ref[0]
    o_ref[...] = (1.0 / (1.0 + jnp.exp(-z))).astype(BF16)


def _gate_proj(x2, g, w, b, tm):
    t = x2.shape[0]
    return pl.pallas_call(
        _gate_kernel,
        out_shape=jax.ShapeDtypeStruct((t, 3 * D_MODEL), BF16),
        grid=(t // tm, 3),
        in_specs=[pl.BlockSpec((tm, D_MODEL), lambda i, j: (i, 0)),
                  pl.BlockSpec((1, D_MODEL), lambda i, j: (0, 0)),
                  pl.BlockSpec((D_MODEL, D_MODEL), lambda i, j: (0, j)),
                  pl.BlockSpec((1, 1, D_MODEL), lambda i, j: (j, 0, 0))],
        out_specs=pl.BlockSpec((tm, D_MODEL), lambda i, j: (i, j)),
        scratch_shapes=[pltpu.VMEM((tm, D_MODEL), BF16)],
        compiler_params=_params(("parallel", "arbitrary")),
        name="gate_proj",
    )(x2, g, w, b)


def _mla_pro_kernel(x_ref, g_ref, wcq_ref, wckv_ref, wkr_ref, gcq_ref, gckv_ref, wuq_ref, wuk_ref, wuv_ref,
                    c_ref, sa_ref, sb_ref, q_ref, k_ref, v_ref, *, qscale):
    h = _rmsnorm(x_ref[...], g_ref[...]).astype(BF16)
    cq = jnp.dot(h, wcq_ref[...], preferred_element_type=F32)
    cq = _rmsnorm(cq, gcq_ref[...]).astype(BF16)
    ckv = jnp.dot(h, wckv_ref[...], preferred_element_type=F32)
    ckv = _rmsnorm(ckv, gckv_ref[...]).astype(BF16)
    c, sa, sb = c_ref[...], sa_ref[...], sb_ref[...]
    half = MLA_ROPE_DIM // 2
    kr = _rope_chunk(jnp.dot(h, wkr_ref[...], preferred_element_type=F32), c, sa, sb, half)
    qh = jnp.dot(cq, wuq_ref[...], preferred_element_type=F32)
    kn = jnp.dot(ckv, wuk_ref[...], preferred_element_type=F32)
    cq_s, sa_s, sb_s = c * qscale, sa * qscale, sb * qscale
    for hd in range(MLA_HEADS):
        sl = slice(hd * MLA_PAD_DIM, (hd + 1) * MLA_PAD_DIM)
        q_ref[:, sl] = _rope_chunk(qh[:, sl], cq_s, sa_s, sb_s, half).astype(BF16)
        k_ref[:, sl] = (kn[:, sl] + kr).astype(BF16)
    v_ref[...] = jnp.dot(ckv, wuv_ref[...], preferred_element_type=F32).astype(BF16)


def _mla_prologue(x2, g, wcq, wckv, wkr, gcq, gckv, wuq, wuk, wuv, tabs, seq, tm):
    t = x2.shape[0]
    nrow = seq // tm
    full = lambda a: pl.BlockSpec(a.shape, lambda i: (0,) * a.ndim)
    tab_spec = pl.BlockSpec((tm, LANES), lambda i: (i % nrow, 0))
    hp = MLA_HEADS * MLA_PAD_DIM
    return pl.pallas_call(
        functools.partial(_mla_pro_kernel, qscale=(MLA_NOPE_DIM + MLA_ROPE_DIM) ** -0.5 * LOG2E),
        out_shape=(jax.ShapeDtypeStruct((t, hp), BF16), jax.ShapeDtypeStruct((t, hp), BF16),
                   jax.ShapeDtypeStruct((t, MLA_WIDTH), BF16)),
        grid=(t // tm,),
        in_specs=[pl.BlockSpec((tm, D_MODEL), lambda i: (i, 0)), full(g), full(wcq), full(wckv), full(wkr),
                  full(gcq), full(gckv), full(wuq), full(wuk), full(wuv), tab_spec, tab_spec, tab_spec],
        out_specs=(pl.BlockSpec((tm, hp), lambda i: (i, 0)), pl.BlockSpec((tm, hp), lambda i: (i, 0)),
                   pl.BlockSpec((tm, MLA_WIDTH), lambda i: (i, 0))),
        compiler_params=_params(("parallel",)),
        name="mla_prologue",
    )(x2, g, wcq, wckv, wkr, gcq, gckv, wuq, wuk, wuv, *tabs)


_NT = (((1,), (1,)), ((), ()))
N_STREAMS = 2


def _chunk(off, tk):
    return pl.ds(off if isinstance(off, int) else pl.multiple_of(off, tk), tk)


def _softmax_phase(st, slot):
    sbuf, pbuf, abuf, m_ref, _ = st
    for z in range(N_STREAMS):
        s = sbuf[z, slot]
        tk = s.shape[1]
        m_prev = m_ref[z]
        m_new = jnp.maximum(m_prev, jnp.max(s, axis=1, keepdims=True))
        p = jnp.exp2(s - jnp.concatenate([m_new] * (tk // LANES), axis=1))
        m_ref[z] = m_new
        abuf[z, slot] = jnp.exp2(m_prev - m_new)
        pbuf[z, slot] = p.astype(BF16)


def _pv_phase(st, v, slot):
    _, pbuf, abuf, _, a_ref = st
    v1 = jnp.concatenate([v, jnp.ones_like(v)], axis=1)
    for z in range(N_STREAMS):
        alpha = abuf[z, slot]
        a_ref[z] = (jnp.concatenate([alpha, alpha], axis=1) * a_ref[z]
                    + jnp.dot(pbuf[z, slot], v1, preferred_element_type=F32))


def _attend(qk_phase, v_of, nk, tk, st):
    assert nk % 2 == 0 and nk >= 2
    m_ref, a_ref = st[3], st[4]
    m_ref[...] = jnp.full(m_ref.shape, NEG_BIG, F32)
    a_ref[...] = jnp.zeros(a_ref.shape, F32)
    qk_phase(0, 0)
    for i in range(nk):
        if i + 1 < nk:
            qk_phase((i + 1) * tk, (i + 1) % 2)
        if i > 0:
            _pv_phase(st, v_of((i - 1) * tk), (i - 1) % 2)
        _softmax_phase(st, i % 2)
    _pv_phase(st, v_of((nk - 1) * tk), (nk - 1) % 2)


def _normalised(st, z):
    acc = st[4][z]
    return acc[:, :LANES] / acc[:, LANES:]


Q_TILES = 2
N_STATE = 5


def _attn_scratch(tq, tk):
    one = [pltpu.VMEM((N_STREAMS, 2, tq, tk), F32), pltpu.VMEM((N_STREAMS, 2, tq, tk), BF16),
           pltpu.VMEM((N_STREAMS, 2, tq, LANES), F32), pltpu.VMEM((N_STREAMS, tq, LANES), F32),
           pltpu.VMEM((N_STREAMS, tq, 2 * LANES), F32)]
    return one * Q_TILES


def _diff_attn_kernel(lam_ref, gd_ref, q_ref, k_ref, v_ref, o_ref, *scratch, tq, tk, lam_init):
    t = lam_ref[...]
    lam = (jnp.exp(jnp.sum(t[0:1] * t[1:2], axis=1, keepdims=True))
           - jnp.exp(jnp.sum(t[2:3] * t[3:4], axis=1, keepdims=True)) + lam_init)
    lane = lax.broadcasted_iota(jnp.int32, (tq, LANES), 1)
    for qt in range(Q_TILES):
        st = scratch[qt * N_STATE:(qt + 1) * N_STATE]
        q = q_ref[0, qt * tq:(qt + 1) * tq, :]
        zero = jnp.zeros_like(q)
        q12 = jnp.concatenate([jnp.where(lane < DIFF_HEAD_DIM, q, zero),
                               jnp.where(lane >= DIFF_HEAD_DIM, q, zero)], axis=0)

        def qk_phase(off, slot, q12=q12, st=st):
            s = lax.dot_general(q12, k_ref[0, _chunk(off, tk), :], _NT, preferred_element_type=F32)
            st[0][0, slot] = s[:tq]
            st[0][1, slot] = s[tq:]

        _attend(qk_phase, lambda off: v_ref[0, _chunk(off, tk), :], k_ref.shape[1] // tk, tk, st)
        o = _normalised(st, 0) - lam * _normalised(st, 1)
        o_ref[0, qt * tq:(qt + 1) * tq, :] = (_rmsnorm(o, gd_ref[...]) * (1.0 - lam_init)).astype(BF16)


def _diff_attention(qkv, lam_p, g_diff, layer_idx, tq, tk):
    b, s, _ = qkv.shape
    kcol, vcol = DIFF_WIDTH // LANES, 2 * DIFF_WIDTH // LANES
    lam_init = 0.8 - 0.6 * math.exp(-0.3 * layer_idx)
    return pl.pallas_call(
        functools.partial(_diff_attn_kernel, tq=tq, tk=tk, lam_init=lam_init),
        out_shape=jax.ShapeDtypeStruct((b, s, DIFF_WIDTH), BF16),
        grid=(b, DIFF_HEADS, s // (Q_TILES * tq)),
        in_specs=[pl.BlockSpec((4, DIFF_HEAD_DIM), lambda bi, h, qi: (0, 0)),
                  pl.BlockSpec((1, DIFF_V_DIM), lambda bi, h, qi: (0, 0)),
                  pl.BlockSpec((1, Q_TILES * tq, LANES), lambda bi, h, qi: (bi, qi, h)),
                  pl.BlockSpec((1, s, LANES), lambda bi, h, qi: (bi, 0, kcol + h)),
                  pl.BlockSpec((1, s, LANES), lambda bi, h, qi: (bi, 0, vcol + h))],
        out_specs=pl.BlockSpec((1, Q_TILES * tq, LANES), lambda bi, h, qi: (bi, qi, h)),
        scratch_shapes=_attn_scratch(tq, tk),
        compiler_params=_params(("parallel", "parallel", "arbitrary")),
        name="diff_attention",
    )(lam_p, g_diff, qkv, qkv, qkv)


def _mla_attn_kernel(q_ref, k_ref, v_ref, o_ref, *scratch, tq, tk):
    lane = lax.broadcasted_iota(jnp.int32, (tq, LANES), 1)
    for qt in range(Q_TILES):
        st = scratch[qt * N_STATE:(qt + 1) * N_STATE]
        rows = slice(qt * tq, (qt + 1) * tq)
        qs = (q_ref[0, rows, :MLA_PAD_DIM], q_ref[0, rows, MLA_PAD_DIM:])

        def qk_phase(off, slot, qs=qs, st=st):
            for z in range(N_STREAMS):
                kz = k_ref[0, _chunk(off, tk), z * MLA_PAD_DIM:(z + 1) * MLA_PAD_DIM]
                st[0][z, slot] = lax.dot_general(qs[z], kz, _NT, preferred_element_type=F32)

        _attend(qk_phase, lambda off: v_ref[0, _chunk(off, tk), :], k_ref.shape[1] // tk, tk, st)
        o_ref[0, rows, :] = jnp.where(lane < MLA_V_DIM, _normalised(st, 0), _normalised(st, 1)).astype(BF16)


def _mla_attention(q, k, v, tq, tk):
    b, s, _ = q.shape
    return pl.pallas_call(
        functools.partial(_mla_attn_kernel, tq=tq, tk=tk),
        out_shape=jax.ShapeDtypeStruct((b, s, MLA_WIDTH), BF16),
        grid=(b, MLA_HEADS // 2, s // (Q_TILES * tq)),
        in_specs=[pl.BlockSpec((1, Q_TILES * tq, 2 * MLA_PAD_DIM), lambda bi, h, qi: (bi, qi, h)),
                  pl.BlockSpec((1, s, 2 * MLA_PAD_DIM), lambda bi, h, qi: (bi, 0, h)),
                  pl.BlockSpec((1, s, LANES), lambda bi, h, qi: (bi, 0, h))],
        out_specs=pl.BlockSpec((1, Q_TILES * tq, LANES), lambda bi, h, qi: (bi, qi, h)),
        scratch_shapes=_attn_scratch(tq, tk),
        compiler_params=_params(("parallel", "parallel", "arbitrary")),
        name="mla_attention",
    )(q, k, v)


def _dil_attn_kernel(q_ref, k_ref, v_ref, o_ref, lse_ref, *, tq, win):
    sub_len = k_ref.shape[1]
    lane = lax.broadcasted_iota(jnp.int32, (tq, DIL_WIDTH), 1)
    heads = [(lane // DIL_HEAD_DIM) == h for h in range(DIL_HEADS)]
    row = lax.broadcasted_iota(jnp.int32, (DIL_HEADS * tq, win), 0) % tq
    col = lax.broadcasted_iota(jnp.int32, (DIL_HEADS * tq, win), 1)
    for t in range(q_ref.shape[1] // tq):
        r0 = pl.program_id(2) * q_ref.shape[1] + t * tq
        start = pl.multiple_of(jnp.clip(r0 - DIL_SPAN, 0, sub_len - win), DIL_SPAN)
        q = q_ref[0, t * tq:(t + 1) * tq, :]
        kw = k_ref[0, pl.ds(start, win), :]
        vw = v_ref[0, pl.ds(start, win), :]
        band = jnp.abs(row - col + (r0 - start)) <= DIL_SPAN
        qz = jnp.concatenate([jnp.where(hm, q, jnp.zeros_like(q)) for hm in heads], axis=0)
        s = jnp.where(band, lax.dot_general(qz, kw, _NT, preferred_element_type=F32), NEG_BIG)
        m = jnp.max(s, axis=1, keepdims=True)
        p = jnp.exp2(s - m)
        l = jnp.sum(p, axis=1, keepdims=True)
        on = jnp.dot(p.astype(BF16), vw, preferred_element_type=F32) / l
        ls = m * LN2 + jnp.log(l)
        o = jnp.zeros(q.shape, F32)
        lse = jnp.zeros(q.shape, F32)
        for h, hm in enumerate(heads):
            o = jnp.where(hm, on[h * tq:(h + 1) * tq], o)
            lse = jnp.where(hm, ls[h * tq:(h + 1) * tq], lse)
        o_ref[0, t * tq:(t + 1) * tq, :] = o
        lse_ref[0, t * tq:(t + 1) * tq, :] = lse


def _dil_attention(qkv_g, dilation, group):
    b, s, _ = qkv_g.shape
    sub_len = s // dilation
    tq = min(128, sub_len)
    win = min(tq + 2 * DIL_SPAN, sub_len)
    rows = min(4 * tq, sub_len)
    sub = qkv_g.reshape(b, sub_len, dilation * 3 * DIL_WIDTH)
    out_sd = jax.ShapeDtypeStruct((b, sub_len, dilation * DIL_WIDTH), F32)
    o, lse = pl.pallas_call(
        functools.partial(_dil_attn_kernel, tq=tq, win=win),
        out_shape=(out_sd, out_sd),
        grid=(b, dilation, sub_len // rows),
        in_specs=[pl.BlockSpec((1, rows, DIL_WIDTH), lambda bi, c, qi: (bi, qi, 3 * c)),
                  pl.BlockSpec((1, sub_len, DIL_WIDTH), lambda bi, c, qi: (bi, 0, 3 * c + 1)),
                  pl.BlockSpec((1, sub_len, DIL_WIDTH), lambda bi, c, qi: (bi, 0, 3 * c + 2))],
        out_specs=(pl.BlockSpec((1, rows, DIL_WIDTH), lambda bi, c, qi: (bi, qi, c)),
                   pl.BlockSpec((1, rows, DIL_WIDTH), lambda bi, c, qi: (bi, qi, c))),
        compiler_params=_params(("parallel", "parallel", "arbitrary")),
        name="dil_attention_%d" % group,
    )(sub, sub, sub)
    return o.reshape(b * s, DIL_WIDTH), lse.reshape(b * s, DIL_WIDTH)


def _merge_kernel(x_ref, oa_ref, oc_ref, o0_ref, o1_ref, o2_ref, e0_ref, e1_ref, e2_ref, gt_ref,
                  wa_ref, wb_ref, wc_ref, wo_ref, y_ref):
    e0, e1, e2 = e0_ref[...], e1_ref[...], e2_ref[...]
    top = jnp.maximum(jnp.maximum(e0, e1), e2)
    p0, p1, p2 = jnp.exp(e0 - top), jnp.exp(e1 - top), jnp.exp(e2 - top)
    ob = (p0 * o0_ref[...] + p1 * o1_ref[...] + p2 * o2_ref[...]) / (p0 + p1 + p2)
    ya = jnp.dot(oa_ref[...], wa_ref[...], preferred_element_type=F32)
    yb = jnp.dot(ob.astype(BF16), wb_ref[...], preferred_element_type=F32)
    yc = jnp.dot(oc_ref[...], wc_ref[...], preferred_element_type=F32)
    merged = (gt_ref[:, :D_MODEL].astype(F32) * ya + gt_ref[:, D_MODEL:2 * D_MODEL].astype(F32) * yb
              + gt_ref[:, 2 * D_MODEL:].astype(F32) * yc)
    y_ref[...] = x_ref[...] + jnp.dot(merged.astype(BF16), wo_ref[...], preferred_element_type=F32)


def _merge_out(x2, oa, oc, dil, gates, wa, wb, wc, wo, tm):
    t = x2.shape[0]
    row = lambda w: pl.BlockSpec((tm, w), lambda i: (i, 0))
    full = lambda a: pl.BlockSpec(a.shape, lambda i: (0, 0))
    (o0, e0), (o1, e1), (o2, e2) = dil
    return pl.pallas_call(
        _merge_kernel,
        out_shape=jax.ShapeDtypeStruct((t, D_MODEL), F32),
        grid=(t // tm,),
        in_specs=[row(D_MODEL), row(DIFF_WIDTH), row(MLA_WIDTH)] + [row(DIL_WIDTH)] * 6 + [row(3 * D_MODEL)]
                 + [full(wa), full(wb), full(wc), full(wo)],
        out_specs=row(D_MODEL),
        compiler_params=_params(("parallel",)),
        name="merge_out",
    )(x2, oa, oc, o0, o1, o2, e0, e1, e2, gates, wa, wb, wc, wo)


def _mlp_kernel(x_ref, g_ref, wu_ref, wd_ref, gf_ref, y_ref, h_ref, acc_ref, *, final_norm):
    j = pl.program_id(1)

    @pl.when(j == 0)
    def _():
        h_ref[...] = _rmsnorm(x_ref[...], g_ref[...]).astype(BF16)
        acc_ref[...] = x_ref[...]

    u = jnp.dot(h_ref[...], wu_ref[...], preferred_element_type=F32)
    u = jnp.square(jnp.maximum(u, 0.0)).astype(BF16)
    acc_ref[...] += jnp.dot(u, wd_ref[...], preferred_element_type=F32)

    @pl.when(j == pl.num_programs(1) - 1)
    def _():
        if final_norm:
            y_ref[...] = _rmsnorm(acc_ref[...], gf_ref[...])
        else:
            y_ref[...] = acc_ref[...]


def _mlp(x2, g, wu, wd, g_final, final_norm, tm, th):
    t = x2.shape[0]
    return pl.pallas_call(
        functools.partial(_mlp_kernel, final_norm=final_norm),
        out_shape=jax.ShapeDtypeStruct((t, D_MODEL), F32),
        grid=(t // tm, MLP_HIDDEN // th),
        in_specs=[pl.BlockSpec((tm, D_MODEL), lambda i, j: (i, 0)),
                  pl.BlockSpec((1, D_MODEL), lambda i, j: (0, 0)),
                  pl.BlockSpec((D_MODEL, th), lambda i, j: (0, j)),
                  pl.BlockSpec((th, D_MODEL), lambda i, j: (j, 0)),
                  pl.BlockSpec((1, D_MODEL), lambda i, j: (0, 0))],
        out_specs=pl.BlockSpec((tm, D_MODEL), lambda i, j: (i, 0)),
        scratch_shapes=[pltpu.VMEM((tm, D_MODEL), BF16), pltpu.VMEM((tm, D_MODEL), F32)],
        compiler_params=_params(("parallel", "arbitrary")),
        name="mlp",
    )(x2, g, wu, wd, g_final)


def _rope_tables(seq, dim, first_lane, period):
    half = dim // 2
    inv = ROPE_THETA ** (-jnp.arange(half, dtype=F32) / half)
    ang = jnp.arange(seq, dtype=F32)[:, None] * inv[None, :]
    lane = jnp.arange(LANES)
    rel = (lane % period) - first_lane
    active = (rel >= 0) & (rel < dim)
    idx = jnp.clip(rel, 0, dim - 1) % half
    cos = jnp.where(active[None, :], jnp.cos(ang)[:, idx], 1.0)
    sin = jnp.where(active[None, :], jnp.sin(ang)[:, idx], 0.0)
    first = (rel < half)[None, :]
    return cos, jnp.where(first, -sin, 0.0), jnp.where(first, 0.0, sin)


def _layer_weights(w_in, w_uq, w_ukv):
    def cols(a, n):
        return w_in[:, a:a + n]
    dq, dk, dv = cols(0, 512), cols(512, 512), cols(1024, 512)
    dil = [[cols(1536 + 768 * g + 256 * t, 256) for t in range(3)] for g in range(3)]
    w_qkv = jnp.concatenate([dq] + [dil[g][0] for g in range(3)] + [dk] + [dil[g][1] for g in range(3)]
                            + [dv] + [dil[g][2] for g in range(3)], axis=1).astype(BF16)
    base = 3840
    w_cq = cols(base, MLA_Q_LORA).astype(BF16)
    w_ckv = cols(base + MLA_Q_LORA, MLA_KV_LORA).astype(BF16)
    w_kr = cols(base + MLA_Q_LORA + MLA_KV_LORA, MLA_ROPE_DIM)
    w_kr = jnp.pad(w_kr, ((0, 0), (MLA_NOPE_DIM, MLA_PAD_DIM - MLA_NOPE_DIM - MLA_ROPE_DIM))).astype(BF16)
    w_gate = cols(base + MLA_Q_LORA + MLA_KV_LORA + MLA_ROPE_DIM, 3 * D_MODEL).astype(BF16)
    qd = MLA_NOPE_DIM + MLA_ROPE_DIM
    w_uq_p = jnp.pad(w_uq.reshape(MLA_Q_LORA, MLA_HEADS, qd), ((0, 0), (0, 0), (0, MLA_PAD_DIM - qd)))
    w_uq_p = w_uq_p.reshape(MLA_Q_LORA, MLA_HEADS * MLA_PAD_DIM).astype(BF16)
    kv = w_ukv.reshape(MLA_KV_LORA, MLA_HEADS, MLA_NOPE_DIM + MLA_V_DIM)
    w_uk_p = jnp.pad(kv[:, :, :MLA_NOPE_DIM], ((0, 0), (0, 0), (0, MLA_PAD_DIM - MLA_NOPE_DIM)))
    w_uk_p = w_uk_p.reshape(MLA_KV_LORA, MLA_HEADS * MLA_PAD_DIM).astype(BF16)
    w_uv = kv[:, :, MLA_NOPE_DIM:].reshape(MLA_KV_LORA, MLA_WIDTH).astype(BF16)
    return w_qkv, w_cq, w_ckv, w_kr, w_gate, w_uq_p, w_uk_p, w_uv


def kernel(x, w_in, b_gate, g_mix, diff_lambda, g_diff, g_cq, g_ckv, w_uq, w_ukv, w_o_diff, w_o_dil, w_o_mla,
           w_out, g_mlp, w_up, w_down, g_final):
    b, s, d = x.shape
    assert d == D_MODEL and s % 1024 == 0
    depth = w_in.shape[0]
    tm = min(1024, s)
    tq, tk = 256, min(512, s)
    tabs64 = _rope_tables(s, DIFF_HEAD_DIM, 0, DIFF_HEAD_DIM)
    tabs_mla = _rope_tables(s, MLA_ROPE_DIM, MLA_NOPE_DIM, MLA_PAD_DIM)
    x2 = x.reshape(b * s, d)
    for l in range(depth):
        w_qkv, w_cq, w_ckv, w_kr, w_gate, w_uq_p, w_uk_p, w_uv = _layer_weights(w_in[l], w_uq[l], w_ukv[l])
        g = g_mix[l][None, :]
        qkv_diff, *qkv_dil = _qkv_proj(x2, g, w_qkv, tabs64, s, tm)
        gates = _gate_proj(x2, g, w_gate, b_gate[l][:, None, :], tm)
        q_c, k_c, v_c = _mla_prologue(x2, g, w_cq, w_ckv, w_kr, g_cq[l][None, :], g_ckv[l][None, :],
                                      w_uq_p, w_uk_p, w_uv, tabs_mla, s, min(512, tm))
        o_a = _diff_attention(qkv_diff.reshape(b, s, 3 * DIFF_WIDTH), diff_lambda[l], g_diff[l][None, :], l, tq, tk)
        dil = [_dil_attention(qkv_dil[gi].reshape(b, s, 3 * DIL_WIDTH), dilation, gi)
               for gi, (_, dilation) in enumerate(DIL_PATTERNS)]
        hp = MLA_HEADS * MLA_PAD_DIM
        o_c = _mla_attention(q_c.reshape(b, s, hp), k_c.reshape(b, s, hp), v_c.reshape(b, s, MLA_WIDTH), tq, tk)
        x2 = _merge_out(x2, o_a.reshape(b * s, DIFF_WIDTH), o_c.reshape(b * s, MLA_WIDTH), dil, gates,
                        w_o_diff[l].astype(BF16), w_o_dil[l].astype(BF16), w_o_mla[l].astype(BF16),
                        w_out[l].astype(BF16), min(512, tm))
        x2 = _mlp(x2, g_mlp[l][None, :], w_up[l].astype(BF16), w_down[l].astype(BF16), g_final[None, :],
                  l == depth - 1, tm, 1024)
    return x2.reshape(b, s, d)
```
